```python
import jax, jax.numpy as jnp
from jax import lax
import numpy as np

D_MODEL = 1024
BATCH = 4
SEQ = 4096
DEPTH = 4

N_HEADS = 16
HEAD_DIM = 64
N_KV_HEADS = 4
ROT_DIM = HEAD_DIM // 4
ROPE_THETA = 500000.0
MOBA_BLOCK = 256
MOBA_TOPK = 3
MOBA_QCHUNK = 32
WINDOW = 128
D_FF = 2816
CONV_W = 3
N_A = DEPTH // 2
N_B = DEPTH - N_A
EPS = 1e-6
NEG = -1e30

kernel_name = 'yoco_moba_swa_sink_convffn'


def rmsnorm(x, g):
    xf = x.astype(jnp.float32)
    y = xf * lax.rsqrt(jnp.mean(xf * xf, axis=-1, keepdims=True) + EPS)
    return (y * g.astype(jnp.float32)).astype(x.dtype)


def rope_tables(positions):
    inv_freq = ROPE_THETA ** (-jnp.arange(0, ROT_DIM, 2, dtype=jnp.float32) / ROT_DIM)
    ang = positions.astype(jnp.float32)[..., None] * inv_freq
    return jnp.cos(ang)[:, :, None, :], jnp.sin(ang)[:, :, None, :]


def partial_rope(x, cos, sin):
    half = ROT_DIM // 2
    x1 = x[..., :half].astype(jnp.float32)
    x2 = x[..., half:ROT_DIM].astype(jnp.float32)
    r1 = (x1 * cos - x2 * sin).astype(x.dtype)
    r2 = (x2 * cos + x1 * sin).astype(x.dtype)
    return jnp.concatenate([r1, r2, x[..., ROT_DIM:]], axis=-1)


def moba_attention(q, k, v):
    B_, H_, S_, dh = q.shape
    nb = -(-S_ // MOBA_BLOCK)
    pad = nb * MOBA_BLOCK - S_
    kp = jnp.pad(k, ((0, 0), (0, 0), (0, pad), (0, 0)))
    vp = jnp.pad(v, ((0, 0), (0, 0), (0, pad), (0, 0)))
    kb = kp.reshape(B_, H_, nb, MOBA_BLOCK, dh)
    vb = vp.reshape(B_, H_, nb, MOBA_BLOCK, dh)
    k_mean = jnp.mean(kb.astype(jnp.float32), axis=3)
    q_blk = jnp.arange(S_) // MOBA_BLOCK
    gate = jnp.einsum('bhsd,bhnd->bhsn', q.astype(jnp.float32), k_mean)
    past = jnp.arange(nb)[None, :] < q_blk[:, None]
    gate = jnp.where(past, gate, NEG)
    n_sel = min(MOBA_TOPK, nb)
    _, sel_idx = lax.top_k(gate, n_sel)
    sel_valid = jnp.arange(n_sel)[None, :] < q_blk[:, None]
    scale = HEAD_DIM ** -0.5
    bi = jnp.arange(B_)[:, None, None, None]
    hi = jnp.arange(H_)[None, :, None, None]
    qpos_local = jnp.arange(MOBA_QCHUNK)
    kpos_local = jnp.arange(MOBA_BLOCK)

    def chunk(c):
        q0 = c * MOBA_QCHUNK
        qc = lax.dynamic_slice_in_dim(q, q0, MOBA_QCHUNK, axis=2)
        idx = lax.dynamic_slice_in_dim(sel_idx, q0, MOBA_QCHUNK, axis=2)
        valid = lax.dynamic_slice_in_dim(sel_valid, q0, MOBA_QCHUNK, axis=0)
        k_g = kb[bi, hi, idx]
        v_g = vb[bi, hi, idx]
        l_sel = jnp.einsum('bhqd,bhqnpd->bhqnp', qc, k_g).astype(jnp.float32) * scale
        l_sel = jnp.where(valid[None, None, :, :, None], l_sel, NEG)
        l_sel = l_sel.reshape(B_, H_, MOBA_QCHUNK, n_sel * MOBA_BLOCK)
        own0 = (q0 // MOBA_BLOCK) * MOBA_BLOCK
        k_own = lax.dynamic_slice_in_dim(kp, own0, MOBA_BLOCK, axis=2)
        v_own = lax.dynamic_slice_in_dim(vp, own0, MOBA_BLOCK, axis=2)
        l_own = jnp.einsum('bhqd,bhpd->bhqp', qc, k_own).astype(jnp.float32) * scale
        causal = (own0 + kpos_local)[None, :] <= (q0 + qpos_local)[:, None]
        l_own = jnp.where(causal, l_own, NEG)
        p = jax.nn.softmax(jnp.concatenate([l_sel, l_own], axis=-1), axis=-1).astype(v.dtype)
        p_sel = p[..., :n_sel * MOBA_BLOCK].reshape(B_, H_, MOBA_QCHUNK, n_sel, MOBA_BLOCK)
        p_own = p[..., n_sel * MOBA_BLOCK:]
        return (jnp.einsum('bhqnp,bhqnpd->bhqd', p_sel, v_g)
                + jnp.einsum('bhqp,bhpd->bhqd', p_own, v_own))

    out = lax.map(chunk, jnp.arange(S_ // MOBA_QCHUNK))
    return out.transpose(1, 2, 0, 3, 4).reshape(B_, H_, S_, dh)


def swa_sink_attention(q, k, v, sinks):
    B_, Hq, S_, dh = q.shape
    G = Hq // N_KV_HEADS
    nq = S_ // WINDOW
    qb = q.reshape(B_, N_KV_HEADS, G, nq, WINDOW, dh)
    kb = k.reshape(B_, N_KV_HEADS, nq, WINDOW, dh)
    vb = v.reshape(B_, N_KV_HEADS, nq, WINDOW, dh)
    kcat = jnp.concatenate([jnp.pad(kb, ((0, 0), (0, 0), (1, 0), (0, 0), (0, 0)))[:, :, :-1], kb], axis=3)
    vcat = jnp.concatenate([jnp.pad(vb, ((0, 0), (0, 0), (1, 0), (0, 0), (0, 0)))[:, :, :-1], vb], axis=3)
    logits = jnp.einsum('bkgnqd,bknpd->bkgnqp', qb, kcat).astype(jnp.float32) * (HEAD_DIM ** -0.5)
    rel = (jnp.arange(WINDOW)[:, None] + WINDOW) - jnp.arange(2 * WINDOW)[None, :]
    band = (rel >= 0) & (rel < WINDOW)
    key_pos = jnp.arange(nq)[:, None, None] * WINDOW - WINDOW + jnp.arange(2 * WINDOW)[None, None, :]
    mask = band[None] & (key_pos >= 0)
    logits = jnp.where(mask, logits, NEG)
    sink = sinks.astype(jnp.float32).reshape(N_KV_HEADS, G)[None, :, :, None, None, None]
    m = jnp.maximum(jnp.max(logits, axis=-1, keepdims=True), sink)
    e = jnp.exp(logits - m)
    p = e / (jnp.sum(e, axis=-1, keepdims=True) + jnp.exp(sink - m))
    out = jnp.einsum('bkgnqp,bknpd->bkgnqd', p.astype(v.dtype), vcat)
    return out.reshape(B_, Hq, S_, dh)


def conv_ffn(h, w_up, conv_w, conv_b, w_down):
    S_ = h.shape[1]
    u = h @ w_up
    up = jnp.pad(u, ((0, 0), (CONV_W - 1, 0), (0, 0)))
    u = conv_b + sum(conv_w[j] * up[:, j:j + S_] for j in range(CONV_W))
    gate, val = jnp.split(u, 2, axis=-1)
    return (jax.nn.silu(gate) * val) @ w_down


def setup_inputs(seed: int = 0) -> dict:
    key = jax.random.key(seed)
    ks = jax.random.split(key, 20)
    D = D_MODEL
    kv_w = 2 * N_KV_HEADS * HEAD_DIM
    res_scale = (2 * DEPTH) ** -0.5

    def nrm(k, shape, fan_in, extra=1.0):
        return jax.random.normal(k, shape, jnp.float32) * (fan_in ** -0.5) * extra

    def gain(k, shape):
        return 1.0 + 0.02 * jax.random.normal(k, shape, jnp.float32)

    offset = jax.random.randint(ks[1], (BATCH, 1), 0, 1024, dtype=jnp.int32)
    positions = offset + jnp.arange(SEQ, dtype=jnp.int32)[None, :]
    return {
        'x': jax.random.normal(ks[0], (BATCH, SEQ, D), jnp.float32),
        'positions': positions,
        'attn_norm': gain(ks[2], (DEPTH, D)),
        'w_qkv_a': nrm(ks[3], (N_A, D, 3 * N_HEADS * HEAD_DIM), D),
        'w_o_a': nrm(ks[4], (N_A, N_HEADS * HEAD_DIM, D), N_HEADS * HEAD_DIM, res_scale),
        'kv_norm': gain(ks[5], (D,)),
        'w_kv_b': nrm(ks[6], (D, kv_w), D),
        'w_q_b': nrm(ks[7], (N_B, D, N_HEADS * HEAD_DIM), D),
        'sinks_b': 0.5 * jax.random.normal(ks[8], (N_B, N_HEADS), jnp.float32),
        'w_o_b': nrm(ks[9], (N_B, N_HEADS * HEAD_DIM, D), N_HEADS * HEAD_DIM, res_scale),
        'ffn_norm': gain(ks[10], (DEPTH, D)),
        'w_up': nrm(ks[11], (DEPTH, D, 2 * D_FF), D),
        'conv_w': nrm(ks[12], (DEPTH, CONV_W, 2 * D_FF), CONV_W),
        'conv_b': 0.01 * jax.random.normal(ks[13], (DEPTH, 2 * D_FF), jnp.float32),
        'w_down': nrm(ks[14], (DEPTH, D_FF, D), D_FF, res_scale),
        'final_norm': gain(ks[15], (D,)),
    }


def reference(x, positions, attn_norm, w_qkv_a, w_o_a, kv_norm, w_kv_b, w_q_b, sinks_b,
              w_o_b, ffn_norm, w_up, conv_w, conv_b, w_down, final_norm):
    B_, S_, _ = x.shape
    cos, sin = rope_tables(positions)

    def heads(t, n):
        return t.reshape(B_, S_, n, HEAD_DIM)

    def to_bhsd(t):
        return t.transpose(0, 2, 1, 3)

    shared_k = None
    shared_v = None
    for l in range(DEPTH):
        if l < N_A:
            h = rmsnorm(x, attn_norm[l])
            q, k, v = jnp.split(h @ w_qkv_a[l], 3, axis=-1)
            q = partial_rope(heads(q, N_HEADS), cos, sin)
            k = partial_rope(heads(k, N_HEADS), cos, sin)
            o = moba_attention(to_bhsd(q), to_bhsd(k), to_bhsd(heads(v, N_HEADS)))
            w_o = w_o_a[l]
        else:
            if l == N_A:
                hk = rmsnorm(x, kv_norm)
                k, v = jnp.split(hk @ w_kv_b, 2, axis=-1)
                shared_k = to_bhsd(partial_rope(heads(k, N_KV_HEADS), cos, sin))
                shared_v = to_bhsd(heads(v, N_KV_HEADS))
            i = l - N_A
            h = rmsnorm(x, attn_norm[l])
            q = partial_rope(heads(h @ w_q_b[i], N_HEADS), cos, sin)
            o = swa_sink_attention(to_bhsd(q), shared_k, shared_v, sinks_b[i])
            w_o = w_o_b[i]
        x = x + o.transpose(0, 2, 1, 3).reshape(B_, S_, N_HEADS * HEAD_DIM) @ w_o
        x = x + conv_ffn(rmsnorm(x, ffn_norm[l]), w_up[l], conv_w[l], conv_b[l], w_down[l])
    return rmsnorm(x, final_norm)
```

```python
import functools

import jax
import jax.numpy as jnp
from jax import lax
from jax.experimental import pallas as pl
from jax.experimental.pallas import tpu as pltpu

D_MODEL = 1024
DEPTH = 4
N_HEADS = 16
HEAD_DIM = 64
N_KV_HEADS = 4
GROUP = N_HEADS // N_KV_HEADS
ROT_DIM = HEAD_DIM // 4
ROPE_THETA = 500000.0
MOBA_BLOCK = 256
MOBA_TOPK = 3
WINDOW = 128
D_FF = 2816
CONV_W = 3
N_A = DEPTH // 2
EPS = 1e-6
NEG = -1e30

LANES = 128
ROW_TILE = 512
PROJ_COL_CHUNK = 512
FF_CHUNK = 256
N_FF_CHUNKS = D_FF // FF_CHUNK
SWA_Q_TILE = 512
VMEM_LIMIT = 56 * 1024 * 1024

_NT = (((1,), (1,)), ((), ()))


def _rms(x, g):
    ms = jnp.mean(x * x, axis=-1, keepdims=True)
    return x * lax.rsqrt(ms + EPS) * g


def _const_spec(shape):
    zeros = (0,) * len(shape)
    return pl.BlockSpec(shape, lambda *_: zeros, pipeline_mode=pl.Buffered(1))


def _proj_kernel(x_ref, g_ref, w_ref, rc_ref, ra_ref, rb_ref, *out_refs,
                 n_out, rope_cols, scale_cols, scale, kmean_cols):
    y_ref = out_refs[0]
    h = _rms(x_ref[0], g_ref[...]).astype(jnp.bfloat16)
    rc = rc_ref[0]
    ra = ra_ref[0]
    rb = rb_ref[0]
    ts = h.shape[0]
    for c0 in range(0, n_out, PROJ_COL_CHUNK):
        width = min(PROJ_COL_CHUNK, n_out - c0)
        y = jnp.dot(h, w_ref[:, c0:c0 + width], preferred_element_type=jnp.float32)
        for l0 in range(0, width, LANES):
            col = c0 + l0
            t = y[:, l0:l0 + LANES]
            if col < rope_cols:
                t = (t * rc + pltpu.roll(t, LANES - ROT_DIM // 2, axis=1) * ra
                     + pltpu.roll(t, ROT_DIM // 2, axis=1) * rb)
            if col < scale_cols:
                t = t * scale
            y_ref[0, :, col:col + LANES] = t.astype(y_ref.dtype)
            if kmean_cols is not None and kmean_cols[0] <= col < kmean_cols[1]:
                km_ref = out_refs[1]
                kc = col - kmean_cols[0]
                for blk in range(ts // MOBA_BLOCK):
                    seg = t[blk * MOBA_BLOCK:(blk + 1) * MOBA_BLOCK]
                    km_ref[0, blk:blk + 1, kc:kc + LANES] = jnp.mean(seg, axis=0, keepdims=True)


def _project(x, g, w, rope, *, rope_cols, scale_cols=0, scale=1.0, kmean_cols=None, name):
    b, s, d = x.shape
    n_out = w.shape[1]
    rc, ra, rb = rope
    blocks_per_tile = ROW_TILE // MOBA_BLOCK
    out_shape = [jax.ShapeDtypeStruct((b, s, n_out), jnp.bfloat16)]
    out_specs = [pl.BlockSpec((1, ROW_TILE, n_out), lambda i, j: (i, j, 0))]
    if kmean_cols is not None:
        kw = kmean_cols[1] - kmean_cols[0]
        out_shape.append(jax.ShapeDtypeStruct((b * (s // ROW_TILE), blocks_per_tile, kw), jnp.float32))
        out_specs.append(pl.BlockSpec((1, blocks_per_tile, kw), lambda i, j: (i * (s // ROW_TILE) + j, 0, 0)))
    tab_spec = pl.BlockSpec((1, ROW_TILE, LANES), lambda i, j: (i, j, 0))
    outs = pl.pallas_call(
        functools.partial(_proj_kernel, n_out=n_out, rope_cols=rope_cols, scale_cols=scale_cols,
                          scale=scale, kmean_cols=kmean_cols),
        grid=(b, s // ROW_TILE),
        in_specs=[
            pl.BlockSpec((1, ROW_TILE, d), lambda i, j: (i, j, 0)),
            _const_spec((1, d)),
            _const_spec((d, n_out)),
            tab_spec, tab_spec, tab_spec,
        ],
        out_specs=out_specs,
        out_shape=out_shape,
        compiler_params=pltpu.CompilerParams(
            dimension_semantics=("parallel", "parallel"), vmem_limit_bytes=VMEM_LIMIT),
        name=name,
    )(x, g.reshape(1, d), w, rc, ra, rb)
    if kmean_cols is not None:
        return outs[0], outs[1].reshape(b, s // MOBA_BLOCK, kmean_cols[1] - kmean_cols[0])
    return outs[0]


def _moba_kernel(q_ref, k_ref, v_ref, km_ref, o_ref, m_ref, l_ref, acc_ref, bits_ref, *, n_blocks):
    i = pl.program_id(2)
    q2 = q_ref[0]
    lane = lax.broadcasted_iota(jnp.int32, (1, LANES), 1)
    head_lanes = (lane < HEAD_DIM, lane >= HEAD_DIM)
    row = lax.broadcasted_iota(jnp.int32, (MOBA_BLOCK, MOBA_BLOCK), 0)
    col = lax.broadcasted_iota(jnp.int32, (MOBA_BLOCK, MOBA_BLOCK), 1)
    causal = col <= row

    blk = lax.broadcasted_iota(jnp.int32, (MOBA_BLOCK, n_blocks), 1)
    past = blk < i
    q2f = q2.astype(jnp.float32)
    for hh in range(2):
        kmh = jnp.where(head_lanes[hh], km_ref[0], 0.0)
        gate = lax.dot_general(q2f, kmh, _NT, precision=lax.Precision.HIGHEST,
                               preferred_element_type=jnp.float32)
        g = jnp.where(past, gate, NEG)
        rank = jnp.zeros((MOBA_BLOCK, n_blocks), jnp.int32)
        for jp in range(n_blocks):
            gj = g[:, jp:jp + 1]
            ahead = (gj > g) | ((gj == g) & (jp < blk))
            rank = rank + ahead.astype(jnp.int32)
        sel = past & (rank < MOBA_TOPK)
        weight = jnp.left_shift(1, blk).astype(jnp.float32)
        bits_ref[hh] = jnp.sum(jnp.where(sel, weight, 0.0), axis=1, keepdims=True).astype(jnp.int32)

    def scores(j, hh):
        start = pl.multiple_of(j * MOBA_BLOCK, MOBA_BLOCK)
        kj = k_ref[0, pl.ds(start, MOBA_BLOCK), :]
        kh = jnp.where(head_lanes[hh], kj, jnp.zeros_like(kj))
        return lax.dot_general(q2, kh, _NT, preferred_element_type=jnp.float32)

    def values(j, hh):
        start = pl.multiple_of(j * MOBA_BLOCK, MOBA_BLOCK)
        vj = v_ref[0, pl.ds(start, MOBA_BLOCK), :]
        return jnp.where(head_lanes[hh], vj, jnp.zeros_like(vj))

    pv = []
    for hh in range(2):
        s = jnp.where(causal, scores(i, hh), NEG)
        m = jnp.max(s, axis=1, keepdims=True)
        p = jnp.exp(s - m)
        m_ref[hh] = m
        l_ref[hh] = jnp.sum(p, axis=1, keepdims=True)
        pv.append(jnp.dot(p.astype(jnp.bfloat16), values(i, hh), preferred_element_type=jnp.float32))
    acc_ref[...] = pv[0] + pv[1]

    def body(j, carry):
        alphas = []
        pvs = []
        for hh in range(2):
            picked = (jnp.right_shift(bits_ref[hh], j) & 1) == 1
            s = jnp.where(picked, scores(j, hh), NEG)
            m_old = m_ref[hh]
            m_new = jnp.maximum(m_old, jnp.max(s, axis=1, keepdims=True))
            alpha = jnp.exp(m_old - m_new)
            p = jnp.exp(s - m_new)
            l_ref[hh] = alpha * l_ref[hh] + jnp.sum(p, axis=1, keepdims=True)
            m_ref[hh] = m_new
            alphas.append(alpha)
            pvs.append(jnp.dot(p.astype(jnp.bfloat16), values(j, hh), preferred_element_type=jnp.float32))
        alpha2 = jnp.where(head_lanes[0], alphas[0], alphas[1])
        acc_ref[...] = acc_ref[...] * alpha2 + pvs[0] + pvs[1]
        return carry

    lax.fori_loop(0, i, body, 0)
    denom = jnp.where(head_lanes[0], l_ref[0], l_ref[1])
    o_ref[0] = (acc_ref[...] / denom).astype(o_ref.dtype)


def _moba_attention(qkv, kmean):
    b, s, _ = qkv.shape
    n_blocks = s // MOBA_BLOCK
    pairs = N_HEADS * HEAD_DIM // LANES
    return pl.pallas_call(
        functools.partial(_moba_kernel, n_blocks=n_blocks),
        grid=(b, pairs, n_blocks),
        in_specs=[
            pl.BlockSpec((1, MOBA_BLOCK, LANES), lambda bi, p, i: (bi, i, p)),
            pl.BlockSpec((1, s, LANES), lambda bi, p, i: (bi, 0, pairs + p)),
            pl.BlockSpec((1, s, LANES), lambda bi, p, i: (bi, 0, 2 * pairs + p)),
            pl.BlockSpec((1, n_blocks, LANES), lambda bi, p, i: (bi, 0, p)),
        ],
        out_specs=pl.BlockSpec((1, MOBA_BLOCK, LANES), lambda bi, p, i: (bi, i, p)),
        out_shape=jax.ShapeDtypeStruct((b, s, N_HEADS * HEAD_DIM), jnp.bfloat16),
        scratch_shapes=[
            pltpu.VMEM((2, MOBA_BLOCK, 1), jnp.float32),
            pltpu.VMEM((2, MOBA_BLOCK, 1), jnp.float32),
            pltpu.VMEM((MOBA_BLOCK, LANES), jnp.float32),
            pltpu.VMEM((2, MOBA_BLOCK, 1), jnp.int32),
        ],
        compiler_params=pltpu.CompilerParams(
            dimension_semantics=("parallel", "parallel", "arbitrary"), vmem_limit_bytes=VMEM_LIMIT),
        name="moba_attention",
    )(qkv, qkv, qkv, kmean)


def _swa_kernel(sink_ref, q_ref, k_ref, v_ref, o_ref):
    a = pl.program_id(1)
    t = pl.program_id(2)
    lane = lax.broadcasted_iota(jnp.int32, (1, LANES), 1)
    half_lanes = (lane < HEAD_DIM, lane >= HEAD_DIM)
    r = lax.broadcasted_iota(jnp.int32, (WINDOW, 2 * WINDOW), 0)
    c = lax.broadcasted_iota(jnp.int32, (WINDOW, 2 * WINDOW), 1)
    band = (c > r) & (c <= r + WINDOW)
    own_only = c >= WINDOW
    for w in range(SWA_Q_TILE // WINDOW):
        n = t * (SWA_Q_TILE // WINDOW) + w
        own0 = pl.multiple_of(n * WINDOW, WINDOW)
        prev0 = pl.multiple_of(jnp.maximum(n - 1, 0) * WINDOW, WINDOW)
        kd = jnp.concatenate([k_ref[0, pl.ds(prev0, WINDOW), :], k_ref[0, pl.ds(own0, WINDOW), :]], axis=0)
        vd = jnp.concatenate([v_ref[0, pl.ds(prev0, WINDOW), :], v_ref[0, pl.ds(own0, WINDOW), :]], axis=0)
        k_half = [jnp.where(half_lanes[e], kd, jnp.zeros_like(kd)) for e in range(2)]
        v_half = [jnp.where(half_lanes[e], vd, jnp.zeros_like(vd)) for e in range(2)]
        mask = band & (own_only | (n > 0))
        for u in range(GROUP * HEAD_DIM // LANES):
            q2 = q_ref[0, w * WINDOW:(w + 1) * WINDOW, u * LANES:(u + 1) * LANES]
            out = None
            for e in range(2):
                sink = sink_ref[a * GROUP + 2 * u + e]
                s = lax.dot_general(q2, k_half[e], _NT, preferred_element_type=jnp.float32)
                s = jnp.where(mask, s, NEG)
                m = jnp.maximum(jnp.max(s, axis=1, keepdims=True), sink)
                p = jnp.exp(s - m)
                denom = jnp.sum(p, axis=1, keepdims=True) + jnp.exp(sink - m)
                pv = jnp.dot(p.astype(jnp.bfloat16), v_half[e], preferred_element_type=jnp.float32) / denom
                out = pv if out is None else out + pv
            o_ref[0, w * WINDOW:(w + 1) * WINDOW, u * LANES:(u + 1) * LANES] = out.astype(o_ref.dtype)


def _swa_attention(q, kv, sinks):
    b, s, _ = q.shape
    gw = GROUP * HEAD_DIM
    grid_spec = pltpu.PrefetchScalarGridSpec(
        num_scalar_prefetch=1,
        grid=(b, N_KV_HEADS, s // SWA_Q_TILE),
        in_specs=[
            pl.BlockSpec((1, SWA_Q_TILE, gw), lambda bi, a, t, sk: (bi, t, a)),
            pl.BlockSpec((1, s, LANES), lambda bi, a, t, sk: (bi, 0, a)),
            pl.BlockSpec((1, s, LANES), lambda bi, a, t, sk: (bi, 0, N_KV_HEADS + a)),
        ],
        out_specs=pl.BlockSpec((1, SWA_Q_TILE, gw), lambda bi, a, t, sk: (bi, t, a)),
    )
    return pl.pallas_call(
        _swa_kernel,
        grid_spec=grid_spec,
        out_shape=jax.ShapeDtypeStruct((b, s, N_HEADS * HEAD_DIM), jnp.bfloat16),
        compiler_params=pltpu.CompilerParams(
            dimension_semantics=("parallel", "parallel", "parallel"), vmem_limit_bytes=VMEM_LIMIT),
        name="swa_attention",
    )(sinks, q, kv, kv)


def _ffn_kernel(x_ref, o_ref, wo_ref, g_ref, wug_ref, wuv_ref, cp_ref, wd_ref, gf_ref, out_ref,
                h_ref, carry_ref, *, final):
    ts = x_ref.shape[1]

    @pl.when(pl.program_id(1) == 0)
    def _():
        carry_ref[...] = jnp.zeros_like(carry_ref)

    x1 = x_ref[0] + jnp.dot(o_ref[0], wo_ref[...], preferred_element_type=jnp.float32)
    out_ref[0] = x1
    h_ref[...] = _rms(x1, g_ref[...]).astype(jnp.bfloat16)
    row = lax.broadcasted_iota(jnp.int32, (ts, 1), 0)

    def conv(u, tail, taps, bias):
        last1 = tail[7:8]
        last2 = tail[6:7]
        u1 = jnp.where(row == 0, last1, pltpu.roll(u, 1, axis=0))
        u2 = jnp.where(row == 0, last2, jnp.where(row == 1, last1, pltpu.roll(u, 2, axis=0)))
        return bias + taps[0:1] * u2 + taps[1:2] * u1 + taps[2:3] * u

    def chunk(c, carry):
        h = h_ref[...]
        cp = cp_ref[c]
        ug = jnp.dot(h, wug_ref[c], preferred_element_type=jnp.float32)
        uv = jnp.dot(h, wuv_ref[c], preferred_element_type=jnp.float32)
        gate = conv(ug, carry_ref[c, 0], cp[0:3], cp[3:4])
        val = conv(uv, carry_ref[c, 1], cp[4:7], cp[7:8])
        carry_ref[c, 0] = ug[ts - 8:]
        carry_ref[c, 1] = uv[ts - 8:]
        act = (gate / (1.0 + jnp.exp(-gate)) * val).astype(jnp.bfloat16)
        out_ref[0] += jnp.dot(act, wd_ref[c], preferred_element_type=jnp.float32)
        return carry

    lax.fori_loop(0, N_FF_CHUNKS, chunk, 0)
    if final:
        out_ref[0] = _rms(out_ref[0], gf_ref[...])


def _attn_out_ffn(x, o, wo, g, wug, wuv, cp, wd, gf, *, final, name):
    b, s, d = x.shape
    row_spec = pl.BlockSpec((1, ROW_TILE, d), lambda i, j: (i, j, 0))
    return pl.pallas_call(
        functools.partial(_ffn_kernel, final=final),
        grid=(b, s // ROW_TILE),
        in_specs=[
            row_spec, row_spec,
            _const_spec((d, d)),
            _const_spec((1, d)),
            _const_spec((N_FF_CHUNKS, d, FF_CHUNK)),
            _const_spec((N_FF_CHUNKS, d, FF_CHUNK)),
            _const_spec((N_FF_CHUNKS, 8, FF_CHUNK)),
            _const_spec((N_FF_CHUNKS, FF_CHUNK, d)),
            _const_spec((1, d)),
        ],
        out_specs=row_spec,
        out_shape=jax.ShapeDtypeStruct((b, s, d), jnp.float32),
        scratch_shapes=[
            pltpu.VMEM((ROW_TILE, d), jnp.bfloat16),
            pltpu.VMEM((N_FF_CHUNKS, 2, 8, FF_CHUNK), jnp.float32),
        ],
        compiler_params=pltpu.CompilerParams(
            dimension_semantics=("parallel", "arbitrary"), vmem_limit_bytes=VMEM_LIMIT),
        name=name,
    )(x, o, wo, g.reshape(1, d), wug, wuv, cp, wd, gf.reshape(1, d))


def _rope_tables(positions):
    inv_freq = ROPE_THETA ** (-jnp.arange(0, ROT_DIM, 2, dtype=jnp.float32) / ROT_DIM)
    ang = positions.astype(jnp.float32)[..., None] * inv_freq
    cos, sin = jnp.cos(ang), jnp.sin(ang)
    half = ROT_DIM // 2
    rest = HEAD_DIM - ROT_DIM
    shape = ang.shape[:-1]
    ones = jnp.ones(shape + (rest,), jnp.float32)
    z_half = jnp.zeros(shape + (half,), jnp.float32)
    z_rest = jnp.zeros(shape + (rest,), jnp.float32)
    rc = jnp.concatenate([cos, cos, ones], axis=-1)
    ra = jnp.concatenate([-sin, z_half, z_rest], axis=-1)
    rb = jnp.concatenate([z_half, sin, z_rest], axis=-1)
    reps = LANES // HEAD_DIM
    return tuple(jnp.tile(t, (1, 1, reps)) for t in (rc, ra, rb))


def _ffn_weights(w_up, conv_w, conv_b, w_down):
    d = w_up.shape[0]

    def chunks(w):
        return w.reshape(d, N_FF_CHUNKS, FF_CHUNK).transpose(1, 0, 2).astype(jnp.bfloat16)

    wug = chunks(w_up[:, :D_FF])
    wuv = chunks(w_up[:, D_FF:])
    rows = jnp.concatenate([conv_w[:, :D_FF], conv_b[None, :D_FF], conv_w[:, D_FF:], conv_b[None, D_FF:]], axis=0)
    cp = rows.reshape(8, N_FF_CHUNKS, FF_CHUNK).transpose(1, 0, 2)
    wd = w_down.reshape(N_FF_CHUNKS, FF_CHUNK, d).astype(jnp.bfloat16)
    return wug, wuv, cp, wd


def _dup_heads(w):
    d = w.shape[0]
    w = w.reshape(d, N_KV_HEADS, 1, HEAD_DIM)
    return jnp.broadcast_to(w, (d, N_KV_HEADS, LANES // HEAD_DIM, HEAD_DIM)).reshape(d, N_KV_HEADS * LANES)


def kernel(x, positions, attn_norm, w_qkv_a, w_o_a, kv_norm, w_kv_b, w_q_b, sinks_b, w_o_b, ffn_norm,
           w_up, conv_w, conv_b, w_down, final_norm):
    assert x.shape[1] % ROW_TILE == 0 and x.shape[1] % SWA_Q_TILE == 0 and x.shape[2] == D_MODEL
    rope = _rope_tables(positions)
    scale = HEAD_DIM ** -0.5
    qw = N_HEADS * HEAD_DIM
    kvw = N_KV_HEADS * HEAD_DIM
    kv = None
    for l in range(DEPTH):
        if l < N_A:
            qkv, kmean = _project(
                x, attn_norm[l], w_qkv_a[l].astype(jnp.bfloat16), rope, rope_cols=2 * qw, scale_cols=qw,
                scale=scale, kmean_cols=(qw, 2 * qw), name=f"proj_qkv_{l}")
            o = _moba_attention(qkv, kmean)
            wo = w_o_a[l]
        else:
            i = l - N_A
            if l == N_A:
                w_kv = jnp.concatenate([_dup_heads(w_kv_b[:, :kvw]), _dup_heads(w_kv_b[:, kvw:])], axis=1)
                kv = _project(x, kv_norm, w_kv.astype(jnp.bfloat16), rope, rope_cols=N_KV_HEADS * LANES,
                              name="proj_kv")
            q = _project(x, attn_norm[l], w_q_b[i].astype(jnp.bfloat16), rope, rope_cols=qw, scale_cols=qw,
                         scale=scale, name=f"proj_q_{l}")
            o = _swa_attention(q, kv, sinks_b[i])
            wo = w_o_b[i]
        wug, wuv, cp, wd = _ffn_weights(w_up[l], conv_w[l], conv_b[l], w_down[l])
        x = _attn_out_ffn(x, o, wo.astype(jnp.bfloat16), ffn_norm[l], wug, wuv, cp, wd, final_norm,
                          final=(l == DEPTH - 1), name=f"attn_out_ffn_{l}")
    return x
```

```python
import functools

import jax
import jax.numpy as jnp
from jax import lax
from jax.experimental import pallas as pl
from jax.experimental.pallas import tpu as pltpu

D_MODEL = 1024
DEPTH = 4
N_HEADS = 16
HEAD_DIM = 64
N_KV_HEADS = 4
GROUP = N_HEADS // N_KV_HEADS
ROT_DIM = HEAD_DIM // 4
ROPE_THETA = 500000.0
MOBA_BLOCK = 256
MOBA_TOPK = 3
WINDOW = 128
D_FF = 2816
CONV_W = 3
N_A = DEPTH // 2
EPS = 1e-6
NEG = -1e30

LANES = 128
ROW_TILE = 512
PROJ_COL_CHUNK = 512
FF_CHUNK = 256
N_FF_CHUNKS = D_FF // FF_CHUNK
SWA_Q_TILE = 512
MOBA_PAIRS = 2
VMEM_LIMIT = 56 * 1024 * 1024

_NT = (((1,), (1,)), ((), ()))


def _rms(x, g):
    ms = jnp.mean(x * x, axis=-1, keepdims=True)
    return x * lax.rsqrt(ms + EPS) * g


def _const_spec(shape):
    zeros = (0,) * len(shape)
    return pl.BlockSpec(shape, lambda *_: zeros, pipeline_mode=pl.Buffered(1))


def _proj_kernel(x_ref, g_ref, w_ref, rc_ref, ra_ref, rb_ref, *out_refs,
                 n_out, rope_cols, scale_cols, scale, kmean_cols):
    y_ref = out_refs[0]
    h = _rms(x_ref[0], g_ref[...]).astype(jnp.bfloat16)
    rc = rc_ref[0]
    ra = ra_ref[0]
    rb = rb_ref[0]
    ts = h.shape[0]
    for c0 in range(0, n_out, PROJ_COL_CHUNK):
        width = min(PROJ_COL_CHUNK, n_out - c0)
        y = jnp.dot(h, w_ref[:, c0:c0 + width], preferred_element_type=jnp.float32)
        for l0 in range(0, width, LANES):
            col = c0 + l0
            t = y[:, l0:l0 + LANES]
            if col < rope_cols:
                t = (t * rc + pltpu.roll(t, LANES - ROT_DIM // 2, axis=1) * ra
                     + pltpu.roll(t, ROT_DIM // 2, axis=1) * rb)
            if col < scale_cols:
                t = t * scale
            y_ref[0, :, col:col + LANES] = t.astype(y_ref.dtype)
            if kmean_cols is not None and kmean_cols[0] <= col < kmean_cols[1]:
                km_ref = out_refs[1]
                kc = col - kmean_cols[0]
                for blk in range(ts // MOBA_BLOCK):
                    seg = t[blk * MOBA_BLOCK:(blk + 1) * MOBA_BLOCK]
                    km_ref[0, blk:blk + 1, kc:kc + LANES] = jnp.mean(seg, axis=0, keepdims=True)


def _project(x, g, w, rope, *, rope_cols, scale_cols=0, scale=1.0, kmean_cols=None, name):
    b, s, d = x.shape
    n_out = w.shape[1]
    rc, ra, rb = rope
    blocks_per_tile = ROW_TILE // MOBA_BLOCK
    out_shape = [jax.ShapeDtypeStruct((b, s, n_out), jnp.bfloat16)]
    out_specs = [pl.BlockSpec((1, ROW_TILE, n_out), lambda i, j: (i, j, 0))]
    if kmean_cols is not None:
        kw = kmean_cols[1] - kmean_cols[0]
        out_shape.append(jax.ShapeDtypeStruct((b * (s // ROW_TILE), blocks_per_tile, kw), jnp.float32))
        out_specs.append(pl.BlockSpec((1, blocks_per_tile, kw), lambda i, j: (i * (s // ROW_TILE) + j, 0, 0)))
    tab_spec = pl.BlockSpec((1, ROW_TILE, LANES), lambda i, j: (i, j, 0))
    outs = pl.pallas_call(
        functools.partial(_proj_kernel, n_out=n_out, rope_cols=rope_cols, scale_cols=scale_cols,
                          scale=scale, kmean_cols=kmean_cols),
        grid=(b, s // ROW_TILE),
        in_specs=[
            pl.BlockSpec((1, ROW_TILE, d), lambda i, j: (i, j, 0)),
            _const_spec((1, d)),
            _const_spec((d, n_out)),
            tab_spec, tab_spec, tab_spec,
        ],
        out_specs=out_specs,
        out_shape=out_shape,
        compiler_params=pltpu.CompilerParams(
            dimension_semantics=("parallel", "parallel"), vmem_limit_bytes=VMEM_LIMIT),
        name=name,
    )(x, g.reshape(1, d), w, rc, ra, rb)
    if kmean_cols is not None:
        return outs[0], outs[1].reshape(b, s // MOBA_BLOCK, kmean_cols[1] - kmean_cols[0])
    return outs[0]


def _moba_kernel(q_ref, k_ref, v_ref, km_ref, o_ref, vt_ref, bias_ref, s_ref, m_ref, l_ref, acc_ref, *,
                 n_blocks, n_pairs):
    i = pl.program_id(2)
    cols = 2 * MOBA_BLOCK

    @pl.when(i == 0)
    def _():
        for c in range(n_blocks):
            blk_v = v_ref[0, c * MOBA_BLOCK:(c + 1) * MOBA_BLOCK, :]
            vt_ref[:, c * MOBA_BLOCK:(c + 1) * MOBA_BLOCK] = blk_v.astype(jnp.float32).T.astype(vt_ref.dtype)

    lane = lax.broadcasted_iota(jnp.int32, (1, LANES), 1)
    head_lanes = (lane < HEAD_DIM, lane >= HEAD_DIM)
    blk = lax.broadcasted_iota(jnp.int32, (n_blocks, cols), 0)
    past = blk < i
    qcat = []
    for pr in range(n_pairs):
        q2 = q_ref[0, :, pr * LANES:(pr + 1) * LANES]
        qc = jnp.concatenate([jnp.where(head_lanes[hh], q2, jnp.zeros_like(q2)) for hh in range(2)], axis=0)
        qcat.append(qc)
        gate = lax.dot_general(km_ref[0, :, pr * LANES:(pr + 1) * LANES], qc.astype(jnp.float32), _NT,
                               precision=lax.Precision.HIGHEST, preferred_element_type=jnp.float32)
        g = jnp.where(past, gate, NEG)
        rank = jnp.zeros((n_blocks, cols), jnp.int32)
        for jp in range(n_blocks):
            gj = g[jp:jp + 1, :]
            ahead = (gj > g) | ((gj == g) & (jp < blk))
            rank = rank + ahead.astype(jnp.int32)
        sel = past & (rank < MOBA_TOPK)
        bias_ref[pr] = jnp.where(sel, 0.0, NEG)
        m_ref[pr] = jnp.full((1, cols), NEG, jnp.float32)
        l_ref[pr] = jnp.zeros((1, cols), jnp.float32)
        acc_ref[pr] = jnp.zeros((HEAD_DIM, cols), jnp.float32)

    def scores(j, pr):
        start = pl.multiple_of(j * MOBA_BLOCK, MOBA_BLOCK)
        kj = k_ref[0, pl.ds(start, MOBA_BLOCK), pr * LANES:(pr + 1) * LANES]
        return lax.dot_general(kj, qcat[pr], _NT, preferred_element_type=jnp.float32)

    def update(j, pr, s, state):
        m_old, l_old, acc_old = state
        m_new = jnp.maximum(m_old, jnp.max(s, axis=0, keepdims=True))
        alpha = jnp.exp(m_old - m_new)
        p = jnp.exp(s - m_new)
        l_new = alpha * l_old + jnp.sum(p, axis=0, keepdims=True)
        pb = p.astype(jnp.bfloat16)
        start = pl.multiple_of(j * MOBA_BLOCK, MOBA_BLOCK)
        pv = []
        for hh in range(2):
            r0 = pr * LANES + hh * HEAD_DIM
            vt = vt_ref[r0:r0 + HEAD_DIM, pl.ds(start, MOBA_BLOCK)]
            pv.append(jnp.dot(vt, pb[:, hh * MOBA_BLOCK:(hh + 1) * MOBA_BLOCK],
                              preferred_element_type=jnp.float32))
        return m_new, l_new, acc_old * alpha + jnp.concatenate(pv, axis=1)

    def load_state(pr):
        return m_ref[pr], l_ref[pr], acc_ref[pr]

    def store_state(pr, state):
        m_ref[pr], l_ref[pr], acc_ref[pr] = state

    for pr in range(n_pairs):
        s_ref[pr] = scores(0, pr)

    def body(j, carry):
        cur = [s_ref[pr] + bias_ref[pr, pl.ds(j, 1), :] for pr in range(n_pairs)]
        states = [load_state(pr) for pr in range(n_pairs)]
        nxt = [scores(j + 1, pr) for pr in range(n_pairs)]
        new = [update(j, pr, cur[pr], states[pr]) for pr in range(n_pairs)]
        for pr in range(n_pairs):
            s_ref[pr] = nxt[pr]
            store_state(pr, new[pr])
        return carry

    lax.fori_loop(0, i, body, 0)

    key_i = lax.broadcasted_iota(jnp.int32, (MOBA_BLOCK, cols), 0)
    qry_i = lax.broadcasted_iota(jnp.int32, (MOBA_BLOCK, cols), 1) & (MOBA_BLOCK - 1)
    causal = key_i <= qry_i
    for pr in range(n_pairs):
        s = jnp.where(causal, s_ref[pr], NEG)
        _, l_fin, acc_fin = update(i, pr, s, load_state(pr))
        out = acc_fin / l_fin
        out_t = jnp.concatenate([out[:, :MOBA_BLOCK], out[:, MOBA_BLOCK:]], axis=0)
        o_ref[0, :, pr * LANES:(pr + 1) * LANES] = out_t.T.astype(o_ref.dtype)


def _moba_attention(qkv, kmean):
    b, s, _ = qkv.shape
    n_blocks = s // MOBA_BLOCK
    width = MOBA_PAIRS * LANES
    groups = N_HEADS * HEAD_DIM // width
    cols = 2 * MOBA_BLOCK
    return pl.pallas_call(
        functools.partial(_moba_kernel, n_blocks=n_blocks, n_pairs=MOBA_PAIRS),
        grid=(b, groups, n_blocks),
        in_specs=[
            pl.BlockSpec((1, MOBA_BLOCK, width), lambda bi, g, i: (bi, i, g)),
            pl.BlockSpec((1, s, width), lambda bi, g, i: (bi, 0, groups + g)),
            pl.BlockSpec((1, s, width), lambda bi, g, i: (bi, 0, 2 * groups + g)),
            pl.BlockSpec((1, n_blocks, width), lambda bi, g, i: (bi, 0, g)),
        ],
        out_specs=pl.BlockSpec((1, MOBA_BLOCK, width), lambda bi, g, i: (bi, i, g)),
        out_shape=jax.ShapeDtypeStruct((b, s, N_HEADS * HEAD_DIM), jnp.bfloat16),
        scratch_shapes=[
            pltpu.VMEM((width, s), jnp.bfloat16),
            pltpu.VMEM((MOBA_PAIRS, n_blocks, cols), jnp.float32),
            pltpu.VMEM((MOBA_PAIRS, MOBA_BLOCK, cols), jnp.float32),
            pltpu.VMEM((MOBA_PAIRS, 1, cols), jnp.float32),
            pltpu.VMEM((MOBA_PAIRS, 1, cols), jnp.float32),
            pltpu.VMEM((MOBA_PAIRS, HEAD_DIM, cols), jnp.float32),
        ],
        compiler_params=pltpu.CompilerParams(
            dimension_semantics=("parallel", "parallel", "arbitrary"), vmem_limit_bytes=VMEM_LIMIT),
        name="moba_attention",
    )(qkv, qkv, qkv, kmean)


def _swa_kernel(sink_ref, q_ref, k_ref, v_ref, o_ref):
    a = pl.program_id(1)
    t = pl.program_id(2)
    lane = lax.broadcasted_iota(jnp.int32, (1, LANES), 1)
    half_lanes = (lane < HEAD_DIM, lane >= HEAD_DIM)
    r = lax.broadcasted_iota(jnp.int32, (WINDOW, 2 * WINDOW), 0)
    c = lax.broadcasted_iota(jnp.int32, (WINDOW, 2 * WINDOW), 1)
    band = (c > r) & (c <= r + WINDOW)
    own_only = c >= WINDOW
    for w in range(SWA_Q_TILE // WINDOW):
        n = t * (SWA_Q_TILE // WINDOW) + w
        own0 = pl.multiple_of(n * WINDOW, WINDOW)
        prev0 = pl.multiple_of(jnp.maximum(n - 1, 0) * WINDOW, WINDOW)
        kd = jnp.concatenate([k_ref[0, pl.ds(prev0, WINDOW), :], k_ref[0, pl.ds(own0, WINDOW), :]], axis=0)
        vd = jnp.concatenate([v_ref[0, pl.ds(prev0, WINDOW), :], v_ref[0, pl.ds(own0, WINDOW), :]], axis=0)
        k_half = [jnp.where(half_lanes[e], kd, jnp.zeros_like(kd)) for e in range(2)]
        v_half = [jnp.where(half_lanes[e], vd, jnp.zeros_like(vd)) for e in range(2)]
        mask = band & (own_only | (n > 0))
        for u in range(GROUP * HEAD_DIM // LANES):
            q2 = q_ref[0, w * WINDOW:(w + 1) * WINDOW, u * LANES:(u + 1) * LANES]
            out = None
            for e in range(2):
                sink = sink_ref[a * GROUP + 2 * u + e]
                s = lax.dot_general(q2, k_half[e], _NT, preferred_element_type=jnp.float32)
                s = jnp.where(mask, s, NEG)
                m = jnp.maximum(jnp.max(s, axis=1, keepdims=True), sink)
                p = jnp.exp(s - m)
                denom = jnp.sum(p, axis=1, keepdims=True) + jnp.exp(sink - m)
                pv = jnp.dot(p.astype(jnp.bfloat16), v_half[e], preferred_element_type=jnp.float32) / denom
                out = pv if out is None else out + pv
            o_ref[0, w * WINDOW:(w + 1) * WINDOW, u * LANES:(u + 1) * LANES] = out.astype(o_ref.dtype)


def _swa_attention(q, kv, sinks):
    b, s, _ = q.shape
    gw = GROUP * HEAD_DIM
    grid_spec = pltpu.PrefetchScalarGridSpec(
        num_scalar_prefetch=1,
        grid=(b, N_KV_HEADS, s // SWA_Q_TILE),
        in_specs=[
            pl.BlockSpec((1, SWA_Q_TILE, gw), lambda bi, a, t, sk: (bi, t, a)),
            pl.BlockSpec((1, s, LANES), lambda bi, a, t, sk: (bi, 0, a)),
            pl.BlockSpec((1, s, LANES), lambda bi, a, t, sk: (bi, 0, N_KV_HEADS + a)),
        ],
        out_specs=pl.BlockSpec((1, SWA_Q_TILE, gw), lambda bi, a, t, sk: (bi, t, a)),
    )
    return pl.pallas_call(
        _swa_kernel,
        grid_spec=grid_spec,
        out_shape=jax.ShapeDtypeStruct((b, s, N_HEADS * HEAD_DIM), jnp.bfloat16),
        compiler_params=pltpu.CompilerParams(
            dimension_semantics=("parallel", "parallel", "parallel"), vmem_limit_bytes=VMEM_LIMIT),
        name="swa_attention",
    )(sinks, q, kv, kv)


def _ffn_kernel(x_ref, o_ref, wo_ref, g_ref, wug_ref, wuv_ref, cp_ref, wd_ref, gf_ref, out_ref,
                h_ref, carry_ref, *, final):
    ts = x_ref.shape[1]

    @pl.when(pl.program_id(1) == 0)
    def _():
        carry_ref[...] = jnp.zeros_like(carry_ref)

    x1 = x_ref[0] + jnp.dot(o_ref[0], wo_ref[...], preferred_element_type=jnp.float32)
    out_ref[0] = x1
    h_ref[...] = _rms(x1, g_ref[...]).astype(jnp.bfloat16)
    row = lax.broadcasted_iota(jnp.int32, (ts, 1), 0)

    def conv(u, tail, taps, bias):
        last1 = tail[7:8]
        last2 = tail[6:7]
        u1 = jnp.where(row == 0, last1, pltpu.roll(u, 1, axis=0))
        u2 = jnp.where(row == 0, last2, jnp.where(row == 1, last1, pltpu.roll(u, 2, axis=0)))
        return bias + taps[0:1] * u2 + taps[1:2] * u1 + taps[2:3] * u

    def chunk(c, carry):
        h = h_ref[...]
        cp = cp_ref[c]
        ug = jnp.dot(h, wug_ref[c], preferred_element_type=jnp.float32)
        uv = jnp.dot(h, wuv_ref[c], preferred_element_type=jnp.float32)
        gate = conv(ug, carry_ref[c, 0], cp[0:3], cp[3:4])
        val = conv(uv, carry_ref[c, 1], cp[4:7], cp[7:8])
        carry_ref[c, 0] = ug[ts - 8:]
        carry_ref[c, 1] = uv[ts - 8:]
        act = (gate / (1.0 + jnp.exp(-gate)) * val).astype(jnp.bfloat16)
        out_ref[0] += jnp.dot(act, wd_ref[c], preferred_element_type=jnp.float32)
        return carry

    lax.fori_loop(0, N_FF_CHUNKS, chunk, 0)
    if final:
        out_ref[0] = _rms(out_ref[0], gf_ref[...])


def _attn_out_ffn(x, o, wo, g, wug, wuv, cp, wd, gf, *, final, name):
    b, s, d = x.shape
    row_spec = pl.BlockSpec((1, ROW_TILE, d), lambda i, j: (i, j, 0))
    return pl.pallas_call(
        functools.partial(_ffn_kernel, final=final),
        grid=(b, s // ROW_TILE),
        in_specs=[
            row_spec, row_spec,
            _const_spec((d, d)),
            _const_spec((1, d)),
            _const_spec((N_FF_CHUNKS, d, FF_CHUNK)),
            _const_spec((N_FF_CHUNKS, d, FF_CHUNK)),
            _const_spec((N_FF_CHUNKS, 8, FF_CHUNK)),
            _const_spec((N_FF_CHUNKS, FF_CHUNK, d)),
            _const_spec((1, d)),
        ],
        out_specs=row_spec,
        out_shape=jax.ShapeDtypeStruct((b, s, d), jnp.float32),
        scratch_shapes=[
            pltpu.VMEM((ROW_TILE, d), jnp.bfloat16),
            pltpu.VMEM((N_FF_CHUNKS, 2, 8, FF_CHUNK), jnp.float32),
        ],
        compiler_params=pltpu.CompilerParams(
            dimension_semantics=("parallel", "arbitrary"), vmem_limit_bytes=VMEM_LIMIT),
        name=name,
    )(x, o, wo, g.reshape(1, d), wug, wuv, cp, wd, gf.reshape(1, d))


def _rope_tables(positions):
    inv_freq = ROPE_THETA ** (-jnp.arange(0, ROT_DIM, 2, dtype=jnp.float32) / ROT_DIM)
    ang = positions.astype(jnp.float32)[..., None] * inv_freq
    cos, sin = jnp.cos(ang), jnp.sin(ang)
    half = ROT_DIM // 2
    rest = HEAD_DIM - ROT_DIM
    shape = ang.shape[:-1]
    ones = jnp.ones(shape + (rest,), jnp.float32)
    z_half = jnp.zeros(shape + (half,), jnp.float32)
    z_rest = jnp.zeros(shape + (rest,), jnp.float32)
    rc = jnp.concatenate([cos, cos, ones], axis=-1)
    ra = jnp.concatenate([-sin, z_half, z_rest], axis=-1)
    rb = jnp.concatenate([z_half, sin, z_rest], axis=-1)
    reps = LANES // HEAD_DIM
    return tuple(jnp.tile(t, (1, 1, reps)) for t in (rc, ra, rb))


def _ffn_weights(w_up, conv_w, conv_b, w_down):
    d = w_up.shape[0]

    def chunks(w):
        return w.reshape(d, N_FF_CHUNKS, FF_CHUNK).transpose(1, 0, 2).astype(jnp.bfloat16)

    wug = chunks(w_up[:, :D_FF])
    wuv = chunks(w_up[:, D_FF:])
    rows = jnp.concatenate([conv_w[:, :D_FF], conv_b[None, :D_FF], conv_w[:, D_FF:], conv_b[None, D_FF:]], axis=0)
    cp = rows.reshape(8, N_FF_CHUNKS, FF_CHUNK).transpose(1, 0, 2)
    wd = w_down.reshape(N_FF_CHUNKS, FF_CHUNK, d).astype(jnp.bfloat16)
    return wug, wuv, cp, wd


def _dup_heads(w):
    d = w.shape[0]
    w = w.reshape(d, N_KV_HEADS, 1, HEAD_DIM)
    return jnp.broadcast_to(w, (d, N_KV_HEADS, LANES // HEAD_DIM, HEAD_DIM)).reshape(d, N_KV_HEADS * LANES)


def kernel(x, positions, attn_norm, w_qkv_a, w_o_a, kv_norm, w_kv_b, w_q_b, sinks_b, w_o_b, ffn_norm,
           w_up, conv_w, conv_b, w_down, final_norm):
    assert x.shape[1] % ROW_TILE == 0 and x.shape[1] % SWA_Q_TILE == 0 and x.shape[2] == D_MODEL
    rope = _rope_tables(positions)
    scale = HEAD_DIM ** -0.5
    qw = N_HEADS * HEAD_DIM
    kvw = N_KV_HEADS * HEAD_DIM
    kv = None
    for l in range(DEPTH):
        if l < N_A:
            qkv, kmean = _project(
                x, attn_norm[l], w_qkv_a[l].astype(jnp.bfloat16), rope, rope_cols=2 * qw, scale_cols=qw,
                scale=scale, kmean_cols=(qw, 2 * qw), name=f"proj_qkv_{l}")
            o = _moba_attention(qkv, kmean)
            wo = w_o_a[l]
        else:
            i = l - N_A
            if l == N_A:
                w_kv = jnp.concatenate([_dup_heads(w_kv_b[:, :kvw]), _dup_heads(w_kv_b[:, kvw:])], axis=1)
                kv = _project(x, kv_norm, w_kv.astype(jnp.bfloat16), rope, rope_cols=N_KV_HEADS * LANES,
                              name="proj_kv")
            q = _project(x, attn_norm[l], w_q_b[i].astype(jnp.bfloat16), rope, rope_cols=qw, scale_cols=qw,
                         scale=scale, name=f"proj_q_{l}")
            o = _swa_attention(q, kv, sinks_b[i])
            wo = w_o_b[i]
        wug, wuv, cp, wd = _ffn_weights(w_up[l], conv_w[l], conv_b[l], w_down[l])
        x = _attn_out_ffn(x, o, wo.astype(jnp.bfloat16), ffn_norm[l], wug, wuv, cp, wd, final_norm,
                          final=(l == DEPTH - 1), name=f"attn_out_ffn_{l}")
    return x
```

```python
import functools

import jax
import jax.numpy as jnp
from jax import lax
from jax.experimental import pallas as pl
from jax.experimental.pallas import tpu as pltpu

D_MODEL = 1024
DEPTH = 4
N_HEADS = 16
HEAD_DIM = 64
N_KV_HEADS = 4
GROUP = N_HEADS // N_KV_HEADS
ROT_DIM = HEAD_DIM // 4
ROPE_THETA = 500000.0
MOBA_BLOCK = 256
MOBA_TOPK = 3
WINDOW = 128
D_FF = 2816
CONV_W = 3
N_A = DEPTH // 2
EPS = 1e-6
NEG = -1e30

LANES = 128
ROW_TILE = 512
PROJ_COL_CHUNK = 512
FF_CHUNK = 256
N_FF_CHUNKS = D_FF // FF_CHUNK
SWA_Q_TILE = 512
MOBA_PAIRS = 2
MOBA_VROWS = HEAD_DIM + 16
VMEM_LIMIT = 56 * 1024 * 1024

_NT = (((1,), (1,)), ((), ()))


def _rms(x, g):
    ms = jnp.mean(x * x, axis=-1, keepdims=True)
    return x * lax.rsqrt(ms + EPS) * g


def _const_spec(shape):
    zeros = (0,) * len(shape)
    return pl.BlockSpec(shape, lambda *_: zeros, pipeline_mode=pl.Buffered(1))


def _proj_kernel(x_ref, g_ref, w_ref, rc_ref, ra_ref, rb_ref, *out_refs,
                 n_out, rope_cols, scale_cols, scale, kmean_cols):
    y_ref = out_refs[0]
    h = _rms(x_ref[0], g_ref[...]).astype(jnp.bfloat16)
    rc = rc_ref[0]
    ra = ra_ref[0]
    rb = rb_ref[0]
    ts = h.shape[0]
    for c0 in range(0, n_out, PROJ_COL_CHUNK):
        width = min(PROJ_COL_CHUNK, n_out - c0)
        y = jnp.dot(h, w_ref[:, c0:c0 + width], preferred_element_type=jnp.float32)
        for l0 in range(0, width, LANES):
            col = c0 + l0
            t = y[:, l0:l0 + LANES]
            if col < rope_cols:
                t = (t * rc + pltpu.roll(t, LANES - ROT_DIM // 2, axis=1) * ra
                     + pltpu.roll(t, ROT_DIM // 2, axis=1) * rb)
            if col < scale_cols:
                t = t * scale
            y_ref[0, :, col:col + LANES] = t.astype(y_ref.dtype)
            if kmean_cols is not None and kmean_cols[0] <= col < kmean_cols[1]:
                km_ref = out_refs[1]
                kc = col - kmean_cols[0]
                for blk in range(ts // MOBA_BLOCK):
                    seg = t[blk * MOBA_BLOCK:(blk + 1) * MOBA_BLOCK]
                    km_ref[0, blk:blk + 1, kc:kc + LANES] = jnp.mean(seg, axis=0, keepdims=True)


def _project(x, g, w, rope, *, rope_cols, scale_cols=0, scale=1.0, kmean_cols=None, name):
    b, s, d = x.shape
    n_out = w.shape[1]
    rc, ra, rb = rope
    blocks_per_tile = ROW_TILE // MOBA_BLOCK
    out_shape = [jax.ShapeDtypeStruct((b, s, n_out), jnp.bfloat16)]
    out_specs = [pl.BlockSpec((1, ROW_TILE, n_out), lambda i, j: (i, j, 0))]
    if kmean_cols is not None:
        kw = kmean_cols[1] - kmean_cols[0]
        out_shape.append(jax.ShapeDtypeStruct((b * (s // ROW_TILE), blocks_per_tile, kw), jnp.float32))
        out_specs.append(pl.BlockSpec((1, blocks_per_tile, kw), lambda i, j: (i * (s // ROW_TILE) + j, 0, 0)))
    tab_spec = pl.BlockSpec((1, ROW_TILE, LANES), lambda i, j: (i, j, 0))
    outs = pl.pallas_call(
        functools.partial(_proj_kernel, n_out=n_out, rope_cols=rope_cols, scale_cols=scale_cols,
                          scale=scale, kmean_cols=kmean_cols),
        grid=(b, s // ROW_TILE),
        in_specs=[
            pl.BlockSpec((1, ROW_TILE, d), lambda i, j: (i, j, 0)),
            _const_spec((1, d)),
            _const_spec((d, n_out)),
            tab_spec, tab_spec, tab_spec,
        ],
        out_specs=out_specs,
        out_shape=out_shape,
        compiler_params=pltpu.CompilerParams(
            dimension_semantics=("parallel", "parallel"), vmem_limit_bytes=VMEM_LIMIT),
        name=name,
    )(x, g.reshape(1, d), w, rc, ra, rb)
    if kmean_cols is not None:
        return outs[0], outs[1].reshape(b, s // MOBA_BLOCK, kmean_cols[1] - kmean_cols[0])
    return outs[0]


def _moba_kernel(q_ref, k_ref, v_ref, km_ref, o_ref, vt_ref, bias_ref, s_ref, p_ref, m_ref, a_ref, acc_ref, *,
                 n_blocks, n_pairs):
    i = pl.program_id(2)
    cols = 2 * MOBA_BLOCK

    @pl.when(i == 0)
    def _():
        ones = jnp.ones((MOBA_VROWS - HEAD_DIM, MOBA_BLOCK), vt_ref.dtype)
        for c in range(n_blocks):
            sl = slice(c * MOBA_BLOCK, (c + 1) * MOBA_BLOCK)
            vt = v_ref[0, sl, :].astype(jnp.float32).T.astype(vt_ref.dtype)
            for hd in range(2 * n_pairs):
                vt_ref[hd * MOBA_VROWS:hd * MOBA_VROWS + HEAD_DIM, sl] = vt[hd * HEAD_DIM:(hd + 1) * HEAD_DIM]
                vt_ref[hd * MOBA_VROWS + HEAD_DIM:(hd + 1) * MOBA_VROWS, sl] = ones

    lane = lax.broadcasted_iota(jnp.int32, (1, LANES), 1)
    head_lanes = (lane < HEAD_DIM, lane >= HEAD_DIM)
    blk = lax.broadcasted_iota(jnp.int32, (n_blocks, cols), 0)
    past = blk < i
    qcat = []
    gates = []
    for pr in range(n_pairs):
        q2 = q_ref[0, :, pr * LANES:(pr + 1) * LANES]
        qc = jnp.concatenate([jnp.where(head_lanes[hh], q2, jnp.zeros_like(q2)) for hh in range(2)], axis=0)
        qcat.append(qc)
        km = km_ref[0, :, pr * LANES:(pr + 1) * LANES]
        hi = km.astype(jnp.bfloat16)
        r1 = km - hi.astype(jnp.float32)
        mid = r1.astype(jnp.bfloat16)
        lo = (r1 - mid.astype(jnp.float32)).astype(jnp.bfloat16)
        parts = lax.dot_general(jnp.concatenate([hi, mid, lo], axis=0), qc, _NT,
                                preferred_element_type=jnp.float32)
        gates.append(parts[:n_blocks] + parts[n_blocks:2 * n_blocks] + parts[2 * n_blocks:])

    def scores(j, pr):
        start = pl.multiple_of(j * MOBA_BLOCK, MOBA_BLOCK)
        kj = k_ref[0, pl.ds(start, MOBA_BLOCK), pr * LANES:(pr + 1) * LANES]
        return lax.dot_general(kj, qcat[pr], _NT, preferred_element_type=jnp.float32)

    def weighted_values(j, pr, pb):
        start = pl.multiple_of(j * MOBA_BLOCK, MOBA_BLOCK)
        outs = []
        for hh in range(2):
            r0 = (2 * pr + hh) * MOBA_VROWS
            vt = vt_ref[r0:r0 + MOBA_VROWS, pl.ds(start, MOBA_BLOCK)]
            outs.append(jnp.dot(vt, pb[:, hh * MOBA_BLOCK:(hh + 1) * MOBA_BLOCK],
                                preferred_element_type=jnp.float32))
        return jnp.concatenate(outs, axis=1)

    key_i = lax.broadcasted_iota(jnp.int32, (MOBA_BLOCK, cols), 0)
    qry_i = lax.broadcasted_iota(jnp.int32, (MOBA_BLOCK, cols), 1) & (MOBA_BLOCK - 1)
    causal = key_i <= qry_i
    own = [scores(i, pr) for pr in range(n_pairs)]
    for pr in range(n_pairs):
        s_ref[pr] = scores(0, pr)
    for pr in range(n_pairs):
        g = jnp.where(past, gates[pr], NEG)
        rank = jnp.zeros((n_blocks, cols), jnp.int32)
        for jp in range(n_blocks):
            gj = g[jp:jp + 1, :]
            ahead = (gj > g) | ((gj == g) & (jp < blk))
            rank = rank + ahead.astype(jnp.int32)
        sel = past & (rank < MOBA_TOPK)
        bias_ref[pr] = jnp.where(sel, 0.0, NEG)
    for pr in range(n_pairs):
        s = jnp.where(causal, own[pr], NEG)
        m = jnp.max(s, axis=0, keepdims=True)
        p_ref[pr] = jnp.exp(s - m).astype(p_ref.dtype)
        m_ref[pr] = m
        a_ref[pr] = jnp.ones((1, cols), jnp.float32)
        acc_ref[pr] = jnp.zeros((MOBA_VROWS, cols), jnp.float32)

    def body(t, carry):
        cur = [s_ref[pr] for pr in range(n_pairs)]
        p_prev = [p_ref[pr] for pr in range(n_pairs)]
        state = [(m_ref[pr], a_ref[pr], acc_ref[pr]) for pr in range(n_pairs)]
        row = [bias_ref[pr, pl.ds(t - 1, 1), :] for pr in range(n_pairs)]
        prev_blk = jnp.where(t == 1, i, t - 2)
        nxt = [scores(t, pr) for pr in range(n_pairs)]
        pv = [weighted_values(prev_blk, pr, p_prev[pr]) for pr in range(n_pairs)]
        for pr in range(n_pairs):
            m_old, a_prev, acc_old = state[pr]
            m_new = jnp.maximum(m_old, jnp.max(cur[pr], axis=0, keepdims=True) + row[pr])
            p_ref[pr] = jnp.exp(cur[pr] - (m_new - row[pr])).astype(p_ref.dtype)
            a_ref[pr] = jnp.exp(m_old - m_new)
            m_ref[pr] = m_new
            acc_ref[pr] = acc_old * a_prev + pv[pr]
            s_ref[pr] = nxt[pr]
        return carry

    lax.fori_loop(1, i + 1, body, 0)

    last_blk = jnp.maximum(i - 1, 0)
    for pr in range(n_pairs):
        acc = acc_ref[pr] * a_ref[pr] + weighted_values(last_blk, pr, p_ref[pr])
        out = acc[:HEAD_DIM] / acc[HEAD_DIM:HEAD_DIM + 1]
        out_t = jnp.concatenate([out[:, :MOBA_BLOCK], out[:, MOBA_BLOCK:]], axis=0)
        o_ref[0, :, pr * LANES:(pr + 1) * LANES] = out_t.T.astype(o_ref.dtype)


def _moba_attention(qkv, kmean):
    b, s, _ = qkv.shape
    n_blocks = s // MOBA_BLOCK
    width = MOBA_PAIRS * LANES
    groups = N_HEADS * HEAD_DIM // width
    cols = 2 * MOBA_BLOCK
    return pl.pallas_call(
        functools.partial(_moba_kernel, n_blocks=n_blocks, n_pairs=MOBA_PAIRS),
        grid=(b, groups, n_blocks),
        in_specs=[
            pl.BlockSpec((1, MOBA_BLOCK, width), lambda bi, g, i: (bi, i, g)),
            pl.BlockSpec((1, s, width), lambda bi, g, i: (bi, 0, groups + g)),
            pl.BlockSpec((1, s, width), lambda bi, g, i: (bi, 0, 2 * groups + g)),
            pl.BlockSpec((1, n_blocks, width), lambda bi, g, i: (bi, 0, g)),
        ],
        out_specs=pl.BlockSpec((1, MOBA_BLOCK, width), lambda bi, g, i: (bi, i, g)),
        out_shape=jax.ShapeDtypeStruct((b, s, N_HEADS * HEAD_DIM), jnp.bfloat16),
        scratch_shapes=[
            pltpu.VMEM((2 * MOBA_PAIRS * MOBA_VROWS, s), jnp.bfloat16),
            pltpu.VMEM((MOBA_PAIRS, n_blocks, cols), jnp.float32),
            pltpu.VMEM((MOBA_PAIRS, MOBA_BLOCK, cols), jnp.float32),
            pltpu.VMEM((MOBA_PAIRS, MOBA_BLOCK, cols), jnp.bfloat16),
            pltpu.VMEM((MOBA_PAIRS, 1, cols), jnp.float32),
            pltpu.VMEM((MOBA_PAIRS, 1, cols), jnp.float32),
            pltpu.VMEM((MOBA_PAIRS, MOBA_VROWS, cols), jnp.float32),
        ],
        compiler_params=pltpu.CompilerParams(
            dimension_semantics=("parallel", "parallel", "arbitrary"), vmem_limit_bytes=VMEM_LIMIT),
        name="moba_attention",
    )(qkv, qkv, qkv, kmean)


def _swa_kernel(sink_ref, q_ref, k_ref, v_ref, o_ref):
    a = pl.program_id(1)
    t = pl.program_id(2)
    lane = lax.broadcasted_iota(jnp.int32, (1, LANES), 1)
    half_lanes = (lane < HEAD_DIM, lane >= HEAD_DIM)
    r = lax.broadcasted_iota(jnp.int32, (WINDOW, 2 * WINDOW), 0)
    c = lax.broadcasted_iota(jnp.int32, (WINDOW, 2 * WINDOW), 1)
    band = (c > r) & (c <= r + WINDOW)
    own_only = c >= WINDOW
    for w in range(SWA_Q_TILE // WINDOW):
        n = t * (SWA_Q_TILE // WINDOW) + w
        own0 = pl.multiple_of(n * WINDOW, WINDOW)
        prev0 = pl.multiple_of(jnp.maximum(n - 1, 0) * WINDOW, WINDOW)
        kd = jnp.concatenate([k_ref[0, pl.ds(prev0, WINDOW), :], k_ref[0, pl.ds(own0, WINDOW), :]], axis=0)
        vd = jnp.concatenate([v_ref[0, pl.ds(prev0, WINDOW), :], v_ref[0, pl.ds(own0, WINDOW), :]], axis=0)
        k_half = [jnp.where(half_lanes[e], kd, jnp.zeros_like(kd)) for e in range(2)]
        v_half = [jnp.where(half_lanes[e], vd, jnp.zeros_like(vd)) for e in range(2)]
        mask = band & (own_only | (n > 0))
        for u in range(GROUP * HEAD_DIM // LANES):
            q2 = q_ref[0, w * WINDOW:(w + 1) * WINDOW, u * LANES:(u + 1) * LANES]
            out = None
            for e in range(2):
                sink = sink_ref[a * GROUP + 2 * u + e]
                s = lax.dot_general(q2, k_half[e], _NT, preferred_element_type=jnp.float32)
                s = jnp.where(mask, s, NEG)
                m = jnp.maximum(jnp.max(s, axis=1, keepdims=True), sink)
                p = jnp.exp(s - m)
                denom = jnp.sum(p, axis=1, keepdims=True) + jnp.exp(sink - m)
                pv = jnp.dot(p.astype(jnp.bfloat16), v_half[e], preferred_element_type=jnp.float32) / denom
                out = pv if out is None else out + pv
            o_ref[0, w * WINDOW:(w + 1) * WINDOW, u * LANES:(u + 1) * LANES] = out.astype(o_ref.dtype)


def _swa_attention(q, kv, sinks):
    b, s, _ = q.shape
    gw = GROUP * HEAD_DIM
    grid_spec = pltpu.PrefetchScalarGridSpec(
        num_scalar_prefetch=1,
        grid=(b, N_KV_HEADS, s // SWA_Q_TILE),
        in_specs=[
            pl.BlockSpec((1, SWA_Q_TILE, gw), lambda bi, a, t, sk: (bi, t, a)),
            pl.BlockSpec((1, s, LANES), lambda bi, a, t, sk: (bi, 0, a)),
            pl.BlockSpec((1, s, LANES), lambda bi, a, t, sk: (bi, 0, N_KV_HEADS + a)),
        ],
        out_specs=pl.BlockSpec((1, SWA_Q_TILE, gw), lambda bi, a, t, sk: (bi, t, a)),
    )
    return pl.pallas_call(
        _swa_kernel,
        grid_spec=grid_spec,
        out_shape=jax.ShapeDtypeStruct((b, s, N_HEADS * HEAD_DIM), jnp.bfloat16),
        compiler_params=pltpu.CompilerParams(
            dimension_semantics=("parallel", "parallel", "parallel"), vmem_limit_bytes=VMEM_LIMIT),
        name="swa_attention",
    )(sinks, q, kv, kv)


def _ffn_kernel(x_ref, o_ref, wo_ref, g_ref, wug_ref, wuv_ref, cp_ref, wd_ref, gf_ref, out_ref,
                h_ref, act_ref, carry_ref, *, final):
    ts = x_ref.shape[1]

    @pl.when(pl.program_id(1) == 0)
    def _():
        carry_ref[...] = jnp.zeros_like(carry_ref)

    x1 = x_ref[0] + jnp.dot(o_ref[0], wo_ref[...], preferred_element_type=jnp.float32)
    out_ref[0] = x1
    h_ref[...] = _rms(x1, g_ref[...]).astype(jnp.bfloat16)
    row = lax.broadcasted_iota(jnp.int32, (ts, 1), 0)

    def conv(u, tail, taps, bias):
        last1 = tail[7:8]
        last2 = tail[6:7]
        u1 = jnp.where(row == 0, last1, pltpu.roll(u, 1, axis=0))
        u2 = jnp.where(row == 0, last2, jnp.where(row == 1, last1, pltpu.roll(u, 2, axis=0)))
        return bias + taps[0:1] * u2 + taps[1:2] * u1 + taps[2:3] * u

    def up(c):
        h = h_ref[...]
        return (jnp.dot(h, wug_ref[c], preferred_element_type=jnp.float32),
                jnp.dot(h, wuv_ref[c], preferred_element_type=jnp.float32))

    def mix(c, ug, uv):
        cp = cp_ref[c]
        gate = conv(ug, carry_ref[c, 0], cp[0:3], cp[3:4])
        val = conv(uv, carry_ref[c, 1], cp[4:7], cp[7:8])
        carry_ref[c, 0] = ug[ts - 8:]
        carry_ref[c, 1] = uv[ts - 8:]
        return (gate / (1.0 + jnp.exp(-gate)) * val).astype(jnp.bfloat16)

    nxt = up(0)
    for c in range(N_FF_CHUNKS):
        cur = nxt
        if c + 1 < N_FF_CHUNKS:
            nxt = up(c + 1)
        act_ref[:, c * FF_CHUNK:(c + 1) * FF_CHUNK] = mix(c, *cur)
    y = out_ref[0] + jnp.dot(act_ref[...], wd_ref[...], preferred_element_type=jnp.float32)
    out_ref[0] = _rms(y, gf_ref[...]) if final else y


def _attn_out_ffn(x, o, wo, g, wug, wuv, cp, wd, gf, *, final, name):
    b, s, d = x.shape
    row_spec = pl.BlockSpec((1, ROW_TILE, d), lambda i, j: (i, j, 0))
    return pl.pallas_call(
        functools.partial(_ffn_kernel, final=final),
        grid=(b, s // ROW_TILE),
        in_specs=[
            row_spec, row_spec,
            _const_spec((d, d)),
            _const_spec((1, d)),
            _const_spec((N_FF_CHUNKS, d, FF_CHUNK)),
            _const_spec((N_FF_CHUNKS, d, FF_CHUNK)),
            _const_spec((N_FF_CHUNKS, 8, FF_CHUNK)),
            _const_spec((D_FF, d)),
            _const_spec((1, d)),
        ],
        out_specs=row_spec,
        out_shape=jax.ShapeDtypeStruct((b, s, d), jnp.float32),
        scratch_shapes=[
            pltpu.VMEM((ROW_TILE, d), jnp.bfloat16),
            pltpu.VMEM((ROW_TILE, D_FF), jnp.bfloat16),
            pltpu.VMEM((N_FF_CHUNKS, 2, 8, FF_CHUNK), jnp.float32),
        ],
        compiler_params=pltpu.CompilerParams(
            dimension_semantics=("parallel", "arbitrary"), vmem_limit_bytes=VMEM_LIMIT),
        name=name,
    )(x, o, wo, g.reshape(1, d), wug, wuv, cp, wd, gf.reshape(1, d))


def _rope_tables(positions):
    inv_freq = ROPE_THETA ** (-jnp.arange(0, ROT_DIM, 2, dtype=jnp.float32) / ROT_DIM)
    ang = positions.astype(jnp.float32)[..., None] * inv_freq
    cos, sin = jnp.cos(ang), jnp.sin(ang)
    half = ROT_DIM // 2
    rest = HEAD_DIM - ROT_DIM
    shape = ang.shape[:-1]
    ones = jnp.ones(shape + (rest,), jnp.float32)
    z_half = jnp.zeros(shape + (half,), jnp.float32)
    z_rest = jnp.zeros(shape + (rest,), jnp.float32)
    rc = jnp.concatenate([cos, cos, ones], axis=-1)
    ra = jnp.concatenate([-sin, z_half, z_rest], axis=-1)
    rb = jnp.concatenate([z_half, sin, z_rest], axis=-1)
    reps = LANES // HEAD_DIM
    return tuple(jnp.tile(t, (1, 1, reps)) for t in (rc, ra, rb))


def _ffn_weights(w_up, conv_w, conv_b, w_down):
    d = w_up.shape[0]

    def chunks(w):
        return w.reshape(d, N_FF_CHUNKS, FF_CHUNK).transpose(1, 0, 2).astype(jnp.bfloat16)

    wug = chunks(w_up[:, :D_FF])
    wuv = chunks(w_up[:, D_FF:])
    rows = jnp.concatenate([conv_w[:, :D_FF], conv_b[None, :D_FF], conv_w[:, D_FF:], conv_b[None, D_FF:]], axis=0)
    cp = rows.reshape(8, N_FF_CHUNKS, FF_CHUNK).transpose(1, 0, 2)
    return wug, wuv, cp, w_down.astype(jnp.bfloat16)


def _dup_heads(w):
    d = w.shape[0]
    w = w.reshape(d, N_KV_HEADS, 1, HEAD_DIM)
    return jnp.broadcast_to(w, (d, N_KV_HEADS, LANES // HEAD_DIM, HEAD_DIM)).reshape(d, N_KV_HEADS * LANES)


def kernel(x, positions, attn_norm, w_qkv_a, w_o_a, kv_norm, w_kv_b, w_q_b, sinks_b, w_o_b, ffn_norm,
           w_up, conv_w, conv_b, w_down, final_norm):
    assert x.shape[1] % ROW_TILE == 0 and x.shape[1] % SWA_Q_TILE == 0 and x.shape[2] == D_MODEL
    rope = _rope_tables(positions)
    scale = HEAD_DIM ** -0.5
    qw = N_HEADS * HEAD_DIM
    kvw = N_KV_HEADS * HEAD_DIM
    kv = None
    for l in range(DEPTH):
        if l < N_A:
            qkv, kmean = _project(
                x, attn_norm[l], w_qkv_a[l].astype(jnp.bfloat16), rope, rope_cols=2 * qw, scale_cols=qw,
                scale=scale, kmean_cols=(qw, 2 * qw), name=f"proj_qkv_{l}")
            o = _moba_attention(qkv, kmean)
            wo = w_o_a[l]
        else:
            i = l - N_A
            if l == N_A:
                w_kv = jnp.concatenate([_dup_heads(w_kv_b[:, :kvw]), _dup_heads(w_kv_b[:, kvw:])], axis=1)
                kv = _project(x, kv_norm, w_kv.astype(jnp.bfloat16), rope, rope_cols=N_KV_HEADS * LANES,
                              name="proj_kv")
            q = _project(x, attn_norm[l], w_q_b[i].astype(jnp.bfloat16), rope, rope_cols=qw, scale_cols=qw,
                         scale=scale, name=f"proj_q_{l}")
            o = _swa_attention(q, kv, sinks_b[i])
            wo = w_o_b[i]
        wug, wuv, cp, wd = _ffn_weights(w_up[l], conv_w[l], conv_b[l], w_down[l])
        x = _attn_out_ffn(x, o, wo.astype(jnp.bfloat16), ffn_norm[l], wug, wuv, cp, wd, final_norm,
                          final=(l == DEPTH - 1), name=f"attn_out_ffn_{l}")
    return x
```

```python
import functools

import jax
import jax.numpy as jnp
from jax import lax
from jax.experimental import pallas as pl
from jax.experimental.pallas import tpu as pltpu

D_MODEL = 1024
DEPTH = 4
N_HEADS = 16
HEAD_DIM = 64
N_KV_HEADS = 4
GROUP = N_HEADS // N_KV_HEADS
ROT_DIM = HEAD_DIM // 4
ROPE_THETA = 500000.0
MOBA_BLOCK = 256
MOBA_TOPK = 3
WINDOW = 128
D_FF = 2816
CONV_W = 3
N_A = DEPTH // 2
EPS = 1e-6
NEG = -1e30

LANES = 128
ROW_TILE = 512
PROJ_COL_CHUNK = 512
FF_CHUNK = 256
N_FF_CHUNKS = D_FF // FF_CHUNK
SWA_Q_TILE = 512
MOBA_PAIRS = 2
MOBA_VROWS = HEAD_DIM + 16
VMEM_LIMIT = 56 * 1024 * 1024

_NT = (((1,), (1,)), ((), ()))


def _rms(x, g):
    ms = jnp.mean(x * x, axis=-1, keepdims=True)
    return x * lax.rsqrt(ms + EPS) * g


def _top16(x):
    bits = lax.bitcast_convert_type(x, jnp.int32) & jnp.int32(-65536)
    return lax.bitcast_convert_type(bits, jnp.float32)


def _const_spec(shape):
    zeros = (0,) * len(shape)
    return pl.BlockSpec(shape, lambda *_: zeros, pipeline_mode=pl.Buffered(1))


def _proj_kernel(x_ref, g_ref, w_ref, rc_ref, ra_ref, rb_ref, *out_refs,
                 n_out, rope_cols, scale_cols, scale, kmean_cols):
    y_ref = out_refs[0]
    h = _rms(x_ref[0], g_ref[...]).astype(jnp.bfloat16)
    rc = rc_ref[0]
    ra = ra_ref[0]
    rb = rb_ref[0]
    ts = h.shape[0]
    for c0 in range(0, n_out, PROJ_COL_CHUNK):
        width = min(PROJ_COL_CHUNK, n_out - c0)
        y = jnp.dot(h, w_ref[:, c0:c0 + width], preferred_element_type=jnp.float32)
        for l0 in range(0, width, LANES):
            col = c0 + l0
            t = y[:, l0:l0 + LANES]
            if col < rope_cols:
                t = (t * rc + pltpu.roll(t, LANES - ROT_DIM // 2, axis=1) * ra
                     + pltpu.roll(t, ROT_DIM // 2, axis=1) * rb)
            if col < scale_cols:
                t = t * scale
            y_ref[0, :, col:col + LANES] = t.astype(y_ref.dtype)
            if kmean_cols is not None and kmean_cols[0] <= col < kmean_cols[1]:
                km_ref = out_refs[1]
                kc = col - kmean_cols[0]
                for blk in range(ts // MOBA_BLOCK):
                    seg = t[blk * MOBA_BLOCK:(blk + 1) * MOBA_BLOCK]
                    km_ref[0, blk:blk + 1, kc:kc + LANES] = jnp.mean(seg, axis=0, keepdims=True)


def _project(x, g, w, rope, *, rope_cols, scale_cols=0, scale=1.0, kmean_cols=None, name):
    b, s, d = x.shape
    n_out = w.shape[1]
    rc, ra, rb = rope
    blocks_per_tile = ROW_TILE // MOBA_BLOCK
    out_shape = [jax.ShapeDtypeStruct((b, s, n_out), jnp.bfloat16)]
    out_specs = [pl.BlockSpec((1, ROW_TILE, n_out), lambda i, j: (i, j, 0))]
    if kmean_cols is not None:
        kw = kmean_cols[1] - kmean_cols[0]
        out_shape.append(jax.ShapeDtypeStruct((b * (s // ROW_TILE), blocks_per_tile, kw), jnp.float32))
        out_specs.append(pl.BlockSpec((1, blocks_per_tile, kw), lambda i, j: (i * (s // ROW_TILE) + j, 0, 0)))
    tab_spec = pl.BlockSpec((1, ROW_TILE, LANES), lambda i, j: (i, j, 0))
    outs = pl.pallas_call(
        functools.partial(_proj_kernel, n_out=n_out, rope_cols=rope_cols, scale_cols=scale_cols,
                          scale=scale, kmean_cols=kmean_cols),
        grid=(b, s // ROW_TILE),
        in_specs=[
            pl.BlockSpec((1, ROW_TILE, d), lambda i, j: (i, j, 0)),
            _const_spec((1, d)),
            _const_spec((d, n_out)),
            tab_spec, tab_spec, tab_spec,
        ],
        out_specs=out_specs,
        out_shape=out_shape,
        compiler_params=pltpu.CompilerParams(
            dimension_semantics=("parallel", "parallel"), vmem_limit_bytes=VMEM_LIMIT),
        name=name,
    )(x, g.reshape(1, d), w, rc, ra, rb)
    if kmean_cols is not None:
        return outs[0], outs[1].reshape(b, s // MOBA_BLOCK, kmean_cols[1] - kmean_cols[0])
    return outs[0]


def _moba_kernel(q_ref, k_ref, v_ref, km_ref, o_ref, vt_ref, bias_ref, s_ref, mx_ref, p_ref, m_ref, a_ref,
                 acc_ref, *, n_blocks, n_pairs):
    i = pl.program_id(2)
    cols = 2 * MOBA_BLOCK

    @pl.when(i == 0)
    def _():
        ones = jnp.ones((MOBA_VROWS - HEAD_DIM, MOBA_BLOCK), vt_ref.dtype)
        for c in range(n_blocks):
            sl = slice(c * MOBA_BLOCK, (c + 1) * MOBA_BLOCK)
            vt = v_ref[0, sl, :].astype(jnp.float32).T.astype(vt_ref.dtype)
            for hd in range(2 * n_pairs):
                vt_ref[hd * MOBA_VROWS:hd * MOBA_VROWS + HEAD_DIM, sl] = vt[hd * HEAD_DIM:(hd + 1) * HEAD_DIM]
                vt_ref[hd * MOBA_VROWS + HEAD_DIM:(hd + 1) * MOBA_VROWS, sl] = ones

    lane = lax.broadcasted_iota(jnp.int32, (1, LANES), 1)
    head_lanes = (lane < HEAD_DIM, lane >= HEAD_DIM)
    blk = lax.broadcasted_iota(jnp.int32, (n_blocks, cols), 0)
    past = blk < i
    qcat = []
    gates = []
    for pr in range(n_pairs):
        q2 = q_ref[0, :, pr * LANES:(pr + 1) * LANES]
        qc = jnp.concatenate([jnp.where(head_lanes[hh], q2, jnp.zeros_like(q2)) for hh in range(2)], axis=0)
        qcat.append(qc)
        km = km_ref[0, :, pr * LANES:(pr + 1) * LANES]
        hi = _top16(km)
        mid = _top16(km - hi)
        lo = km - hi - mid
        limbs = jnp.concatenate([hi, mid, lo], axis=0).astype(jnp.bfloat16)
        parts = lax.dot_general(limbs, qc, _NT, preferred_element_type=jnp.float32)
        gates.append(parts[:n_blocks] + parts[n_blocks:2 * n_blocks] + parts[2 * n_blocks:])

    def scores(j, pr):
        start = pl.multiple_of(j * MOBA_BLOCK, MOBA_BLOCK)
        kj = k_ref[0, pl.ds(start, MOBA_BLOCK), pr * LANES:(pr + 1) * LANES]
        return lax.dot_general(kj, qcat[pr], _NT, preferred_element_type=jnp.float32)

    def weighted_values(j, pr, pb):
        start = pl.multiple_of(j * MOBA_BLOCK, MOBA_BLOCK)
        outs = []
        for hh in range(2):
            r0 = (2 * pr + hh) * MOBA_VROWS
            vt = vt_ref[r0:r0 + MOBA_VROWS, pl.ds(start, MOBA_BLOCK)]
            outs.append(jnp.dot(vt, pb[:, hh * MOBA_BLOCK:(hh + 1) * MOBA_BLOCK],
                                preferred_element_type=jnp.float32))
        return jnp.concatenate(outs, axis=1)

    def stage(pr, sc):
        s_ref[pr] = sc
        mx_ref[pr] = jnp.max(sc, axis=0, keepdims=True)

    key_i = lax.broadcasted_iota(jnp.int32, (MOBA_BLOCK, cols), 0)
    qry_i = lax.broadcasted_iota(jnp.int32, (MOBA_BLOCK, cols), 1) & (MOBA_BLOCK - 1)
    causal = key_i <= qry_i
    own = [scores(i, pr) for pr in range(n_pairs)]
    for pr in range(n_pairs):
        stage(pr, scores(0, pr))
    for pr in range(n_pairs):
        g = jnp.where(past, gates[pr], NEG)
        rank = jnp.zeros((n_blocks, cols), jnp.int32)
        for jp in range(n_blocks):
            gj = g[jp:jp + 1, :]
            ahead = (gj > g) | ((gj == g) & (jp < blk))
            rank = rank + ahead.astype(jnp.int32)
        sel = past & (rank < MOBA_TOPK)
        bias_ref[pr] = jnp.where(sel, 0.0, NEG)
    for pr in range(n_pairs):
        s = jnp.where(causal, own[pr], NEG)
        m = jnp.max(s, axis=0, keepdims=True)
        p_ref[pr] = jnp.exp(s - m).astype(p_ref.dtype)
        m_ref[pr] = m
        a_ref[pr] = jnp.ones((1, cols), jnp.float32)
        acc_ref[pr] = jnp.zeros((MOBA_VROWS, cols), jnp.float32)

    def body(t, carry):
        cur = [s_ref[pr] for pr in range(n_pairs)]
        p_prev = [p_ref[pr] for pr in range(n_pairs)]
        state = [(m_ref[pr], a_ref[pr], acc_ref[pr], mx_ref[pr]) for pr in range(n_pairs)]
        row = [bias_ref[pr, pl.ds(t - 1, 1), :] for pr in range(n_pairs)]
        prev_blk = jnp.where(t == 1, i, t - 2)
        nxt = [scores(t, pr) for pr in range(n_pairs)]
        pv = [weighted_values(prev_blk, pr, p_prev[pr]) for pr in range(n_pairs)]
        for pr in range(n_pairs):
            m_old, a_prev, acc_old, cur_max = state[pr]
            m_new = jnp.maximum(m_old, jnp.max(cur[pr], axis=0, keepdims=True) + row[pr])
            p_ref[pr] = jnp.exp(cur[pr] - (m_new - row[pr])).astype(p_ref.dtype)
            a_ref[pr] = jnp.exp(m_old - m_new)
            m_ref[pr] = m_new
            acc_ref[pr] = acc_old * a_prev + pv[pr]
            stage(pr, nxt[pr])
        return carry

    lax.fori_loop(1, i + 1, body, 0)

    last_blk = jnp.maximum(i - 1, 0)
    for pr in range(n_pairs):
        acc = acc_ref[pr] * a_ref[pr] + weighted_values(last_blk, pr, p_ref[pr])
        out = acc[:HEAD_DIM] / acc[HEAD_DIM:HEAD_DIM + 1]
        out_t = jnp.concatenate([out[:, :MOBA_BLOCK], out[:, MOBA_BLOCK:]], axis=0)
        o_ref[0, :, pr * LANES:(pr + 1) * LANES] = out_t.T.astype(o_ref.dtype)


def _moba_attention(qkv, kmean):
    b, s, _ = qkv.shape
    n_blocks = s // MOBA_BLOCK
    width = MOBA_PAIRS * LANES
    groups = N_HEADS * HEAD_DIM // width
    cols = 2 * MOBA_BLOCK
    return pl.pallas_call(
        functools.partial(_moba_kernel, n_blocks=n_blocks, n_pairs=MOBA_PAIRS),
        grid=(b, groups, n_blocks),
        in_specs=[
            pl.BlockSpec((1, MOBA_BLOCK, width), lambda bi, g, i: (bi, i, g)),
            pl.BlockSpec((1, s, width), lambda bi, g, i: (bi, 0, groups + g)),
            pl.BlockSpec((1, s, width), lambda bi, g, i: (bi, 0, 2 * groups + g)),
            pl.BlockSpec((1, n_blocks, width), lambda bi, g, i: (bi, 0, g)),
        ],
        out_specs=pl.BlockSpec((1, MOBA_BLOCK, width), lambda bi, g, i: (bi, i, g)),
        out_shape=jax.ShapeDtypeStruct((b, s, N_HEADS * HEAD_DIM), jnp.bfloat16),
        scratch_shapes=[
            pltpu.VMEM((2 * MOBA_PAIRS * MOBA_VROWS, s), jnp.bfloat16),
            pltpu.VMEM((MOBA_PAIRS, n_blocks, cols), jnp.float32),
            pltpu.VMEM((MOBA_PAIRS, MOBA_BLOCK, cols), jnp.float32),
            pltpu.VMEM((MOBA_PAIRS, 1, cols), jnp.float32),
            pltpu.VMEM((MOBA_PAIRS, MOBA_BLOCK, cols), jnp.bfloat16),
            pltpu.VMEM((MOBA_PAIRS, 1, cols), jnp.float32),
            pltpu.VMEM((MOBA_PAIRS, 1, cols), jnp.float32),
            pltpu.VMEM((MOBA_PAIRS, MOBA_VROWS, cols), jnp.float32),
        ],
        compiler_params=pltpu.CompilerParams(
            dimension_semantics=("parallel", "parallel", "arbitrary"), vmem_limit_bytes=VMEM_LIMIT),
        name="moba_attention",
    )(qkv, qkv, qkv, kmean)


def _swa_kernel(sink_ref, q_ref, k_ref, v_ref, o_ref):
    a = pl.program_id(1)
    t = pl.program_id(2)
    lane = lax.broadcasted_iota(jnp.int32, (1, LANES), 1)
    half_lanes = (lane < HEAD_DIM, lane >= HEAD_DIM)
    r = lax.broadcasted_iota(jnp.int32, (WINDOW, 2 * WINDOW), 0)
    c = lax.broadcasted_iota(jnp.int32, (WINDOW, 2 * WINDOW), 1)
    band = (c > r) & (c <= r + WINDOW)
    own_only = c >= WINDOW
    for w in range(SWA_Q_TILE // WINDOW):
        n = t * (SWA_Q_TILE // WINDOW) + w
        own0 = pl.multiple_of(n * WINDOW, WINDOW)
        prev0 = pl.multiple_of(jnp.maximum(n - 1, 0) * WINDOW, WINDOW)
        kd = jnp.concatenate([k_ref[0, pl.ds(prev0, WINDOW), :], k_ref[0, pl.ds(own0, WINDOW), :]], axis=0)
        vd = jnp.concatenate([v_ref[0, pl.ds(prev0, WINDOW), :], v_ref[0, pl.ds(own0, WINDOW), :]], axis=0)
        k_half = [jnp.where(half_lanes[e], kd, jnp.zeros_like(kd)) for e in range(2)]
        v_half = [jnp.where(half_lanes[e], vd, jnp.zeros_like(vd)) for e in range(2)]
        mask = band & (own_only | (n > 0))
        for u in range(GROUP * HEAD_DIM // LANES):
            q2 = q_ref[0, w * WINDOW:(w + 1) * WINDOW, u * LANES:(u + 1) * LANES]
            out = None
            for e in range(2):
                sink = sink_ref[a * GROUP + 2 * u + e]
                s = lax.dot_general(q2, k_half[e], _NT, preferred_element_type=jnp.float32)
                s = jnp.where(mask, s, NEG)
                m = jnp.maximum(jnp.max(s, axis=1, keepdims=True), sink)
                p = jnp.exp(s - m)
                denom = jnp.sum(p, axis=1, keepdims=True) + jnp.exp(sink - m)
                pv = jnp.dot(p.astype(jnp.bfloat16), v_half[e], preferred_element_type=jnp.float32) / denom
                out = pv if out is None else out + pv
            o_ref[0, w * WINDOW:(w + 1) * WINDOW, u * LANES:(u + 1) * LANES] = out.astype(o_ref.dtype)


def _swa_attention(q, kv, sinks):
    b, s, _ = q.shape
    gw = GROUP * HEAD_DIM
    grid_spec = pltpu.PrefetchScalarGridSpec(
        num_scalar_prefetch=1,
        grid=(b, N_KV_HEADS, s // SWA_Q_TILE),
        in_specs=[
            pl.BlockSpec((1, SWA_Q_TILE, gw), lambda bi, a, t, sk: (bi, t, a)),
            pl.BlockSpec((1, s, LANES), lambda bi, a, t, sk: (bi, 0, a)),
            pl.BlockSpec((1, s, LANES), lambda bi, a, t, sk: (bi, 0, N_KV_HEADS + a)),
        ],
        out_specs=pl.BlockSpec((1, SWA_Q_TILE, gw), lambda bi, a, t, sk: (bi, t, a)),
    )
    return pl.pallas_call(
        _swa_kernel,
        grid_spec=grid_spec,
        out_shape=jax.ShapeDtypeStruct((b, s, N_HEADS * HEAD_DIM), jnp.bfloat16),
        compiler_params=pltpu.CompilerParams(
            dimension_semantics=("parallel", "parallel", "parallel"), vmem_limit_bytes=VMEM_LIMIT),
        name="swa_attention",
    )(sinks, q, kv, kv)


def _ffn_kernel(x_ref, o_ref, wo_ref, g_ref, wu_ref, cw_ref, cb_ref, wd_ref, gf_ref, out_ref,
                h_ref, act_ref, carry_ref, *, final):
    ts = x_ref.shape[1]

    @pl.when(pl.program_id(1) == 0)
    def _():
        carry_ref[...] = jnp.zeros_like(carry_ref)

    x1 = x_ref[0] + jnp.dot(o_ref[0], wo_ref[...], preferred_element_type=jnp.float32)
    out_ref[0] = x1
    h_ref[...] = _rms(x1, g_ref[...]).astype(jnp.bfloat16)
    row = lax.broadcasted_iota(jnp.int32, (ts, 1), 0)

    def conv(u, tail, taps, bias):
        last1 = tail[7:8]
        last2 = tail[6:7]
        u1 = jnp.where(row == 0, last1, pltpu.roll(u, 1, axis=0))
        u2 = jnp.where(row == 0, last2, jnp.where(row == 1, last1, pltpu.roll(u, 2, axis=0)))
        return bias + taps[0:1] * u2 + taps[1:2] * u1 + taps[2:3] * u

    def cols(c, half):
        start = half * D_FF + c * FF_CHUNK
        return slice(start, start + FF_CHUNK)

    def up(c):
        h = h_ref[...]
        return tuple(jnp.dot(h, wu_ref[:, cols(c, half)], preferred_element_type=jnp.float32)
                     for half in range(2))

    def mix(c, ug, uv):
        gate = conv(ug, carry_ref[c, 0], cw_ref[:, cols(c, 0)], cb_ref[:, cols(c, 0)])
        val = conv(uv, carry_ref[c, 1], cw_ref[:, cols(c, 1)], cb_ref[:, cols(c, 1)])
        carry_ref[c, 0] = ug[ts - 8:]
        carry_ref[c, 1] = uv[ts - 8:]
        return (gate / (1.0 + jnp.exp(-gate)) * val).astype(jnp.bfloat16)

    nxt = up(0)
    for c in range(N_FF_CHUNKS):
        cur = nxt
        if c + 1 < N_FF_CHUNKS:
            nxt = up(c + 1)
        act_ref[:, c * FF_CHUNK:(c + 1) * FF_CHUNK] = mix(c, *cur)
    y = out_ref[0] + jnp.dot(act_ref[...], wd_ref[...], preferred_element_type=jnp.float32)
    out_ref[0] = _rms(y, gf_ref[...]) if final else y


def _attn_out_ffn(x, o, wo, g, wu, cw, cb, wd, gf, *, final, name):
    b, s, d = x.shape
    row_spec = pl.BlockSpec((1, ROW_TILE, d), lambda i, j: (i, j, 0))
    return pl.pallas_call(
        functools.partial(_ffn_kernel, final=final),
        grid=(b, s // ROW_TILE),
        in_specs=[
            row_spec, row_spec,
            _const_spec((d, d)),
            _const_spec((1, d)),
            _const_spec((d, 2 * D_FF)),
            _const_spec((CONV_W, 2 * D_FF)),
            _const_spec((1, 2 * D_FF)),
            _const_spec((D_FF, d)),
            _const_spec((1, d)),
        ],
        out_specs=row_spec,
        out_shape=jax.ShapeDtypeStruct((b, s, d), jnp.float32),
        scratch_shapes=[
            pltpu.VMEM((ROW_TILE, d), jnp.bfloat16),
            pltpu.VMEM((ROW_TILE, D_FF), jnp.bfloat16),
            pltpu.VMEM((N_FF_CHUNKS, 2, 8, FF_CHUNK), jnp.float32),
        ],
        compiler_params=pltpu.CompilerParams(
            dimension_semantics=("parallel", "arbitrary"), vmem_limit_bytes=VMEM_LIMIT),
        name=name,
    )(x, o, wo, g.reshape(1, d), wu, cw, cb.reshape(1, 2 * D_FF), wd, gf.reshape(1, d))


def _rope_tables(positions):
    inv_freq = ROPE_THETA ** (-jnp.arange(0, ROT_DIM, 2, dtype=jnp.float32) / ROT_DIM)
    ang = positions.astype(jnp.float32)[..., None] * inv_freq
    cos, sin = jnp.cos(ang), jnp.sin(ang)
    half = ROT_DIM // 2
    rest = HEAD_DIM - ROT_DIM
    shape = ang.shape[:-1]
    ones = jnp.ones(shape + (rest,), jnp.float32)
    z_half = jnp.zeros(shape + (half,), jnp.float32)
    z_rest = jnp.zeros(shape + (rest,), jnp.float32)
    rc = jnp.concatenate([cos, cos, ones], axis=-1)
    ra = jnp.concatenate([-sin, z_half, z_rest], axis=-1)
    rb = jnp.concatenate([z_half, sin, z_rest], axis=-1)
    reps = LANES // HEAD_DIM
    return tuple(jnp.tile(t, (1, 1, reps)) for t in (rc, ra, rb))


def _dup_heads(w):
    d = w.shape[0]
    w = w.reshape(d, N_KV_HEADS, 1, HEAD_DIM)
    return jnp.broadcast_to(w, (d, N_KV_HEADS, LANES // HEAD_DIM, HEAD_DIM)).reshape(d, N_KV_HEADS * LANES)


def kernel(x, positions, attn_norm, w_qkv_a, w_o_a, kv_norm, w_kv_b, w_q_b, sinks_b, w_o_b, ffn_norm,
           w_up, conv_w, conv_b, w_down, final_norm):
    assert x.shape[1] % ROW_TILE == 0 and x.shape[1] % SWA_Q_TILE == 0 and x.shape[2] == D_MODEL
    rope = _rope_tables(positions)
    scale = HEAD_DIM ** -0.5
    qw = N_HEADS * HEAD_DIM
    kvw = N_KV_HEADS * HEAD_DIM
    kv = None
    for l in range(DEPTH):
        if l < N_A:
            qkv, kmean = _project(
                x, attn_norm[l], w_qkv_a[l].astype(jnp.bfloat16), rope, rope_cols=2 * qw, scale_cols=qw,
                scale=scale, kmean_cols=(qw, 2 * qw), name=f"proj_qkv_{l}")
            o = _moba_attention(qkv, kmean)
            wo = w_o_a[l]
        else:
            i = l - N_A
            if l == N_A:
                w_kv = jnp.concatenate([_dup_heads(w_kv_b[:, :kvw]), _dup_heads(w_kv_b[:, kvw:])], axis=1)
                kv = _project(x, kv_norm, w_kv.astype(jnp.bfloat16), rope, rope_cols=N_KV_HEADS * LANES,
                              name="proj_kv")
            q = _project(x, attn_norm[l], w_q_b[i].astype(jnp.bfloat16), rope, rope_cols=qw, scale_cols=qw,
                         scale=scale, name=f"proj_q_{l}")
            o = _swa_attention(q, kv, sinks_b[i])
            wo = w_o_b[i]
        x = _attn_out_ffn(x, o, wo.astype(jnp.bfloat16), ffn_norm[l], w_up[l].astype(jnp.bfloat16), conv_w[l],
                          conv_b[l], w_down[l].astype(jnp.bfloat16), final_norm,
                          final=(l == DEPTH - 1), name=f"attn_out_ffn_{l}")
    return x
```

```python
import functools

import jax
import jax.numpy as jnp
from jax import lax
from jax.experimental import pallas as pl
from jax.experimental.pallas import tpu as pltpu

D_MODEL = 1024
DEPTH = 4
N_HEADS = 16
HEAD_DIM = 64
N_KV_HEADS = 4
GROUP = N_HEADS // N_KV_HEADS
ROT_DIM = HEAD_DIM // 4
ROPE_THETA = 500000.0
MOBA_BLOCK = 256
MOBA_TOPK = 3
WINDOW = 128
D_FF = 2816
CONV_W = 3
N_A = DEPTH // 2
EPS = 1e-6
NEG = -1e30

LANES = 128
ROW_TILE = 512
PROJ_COL_CHUNK = 512
FF_CHUNK = 256
N_FF_CHUNKS = D_FF // FF_CHUNK
SWA_Q_TILE = 512
MOBA_PAIRS = 2
MOBA_VROWS = HEAD_DIM + 16
VMEM_LIMIT = 56 * 1024 * 1024

_NT = (((1,), (1,)), ((), ()))


def _rms(x, g):
    ms = jnp.mean(x * x, axis=-1, keepdims=True)
    return x * lax.rsqrt(ms + EPS) * g


def _top16(x):
    bits = lax.bitcast_convert_type(x, jnp.int32) & jnp.int32(-65536)
    return lax.bitcast_convert_type(bits, jnp.float32)


def _const_spec(shape):
    zeros = (0,) * len(shape)
    return pl.BlockSpec(shape, lambda *_: zeros, pipeline_mode=pl.Buffered(1))


def _proj_kernel(x_ref, g_ref, w_ref, rc_ref, ra_ref, rb_ref, *out_refs,
                 n_out, rope_cols, scale_cols, scale, kmean_cols):
    y_ref = out_refs[0]
    h = _rms(x_ref[0], g_ref[...]).astype(jnp.bfloat16)
    rc = rc_ref[0]
    ra = ra_ref[0]
    rb = rb_ref[0]
    ts = h.shape[0]
    for c0 in range(0, n_out, PROJ_COL_CHUNK):
        width = min(PROJ_COL_CHUNK, n_out - c0)
        y = jnp.dot(h, w_ref[:, c0:c0 + width], preferred_element_type=jnp.float32)
        for l0 in range(0, width, LANES):
            col = c0 + l0
            t = y[:, l0:l0 + LANES]
            if col < rope_cols:
                t = (t * rc + pltpu.roll(t, LANES - ROT_DIM // 2, axis=1) * ra
                     + pltpu.roll(t, ROT_DIM // 2, axis=1) * rb)
            if col < scale_cols:
                t = t * scale
            y_ref[0, :, col:col + LANES] = t.astype(y_ref.dtype)
            if kmean_cols is not None and kmean_cols[0] <= col < kmean_cols[1]:
                km_ref = out_refs[1]
                kc = col - kmean_cols[0]
                for blk in range(ts // MOBA_BLOCK):
                    seg = t[blk * MOBA_BLOCK:(blk + 1) * MOBA_BLOCK]
                    km_ref[0, blk:blk + 1, kc:kc + LANES] = jnp.mean(seg, axis=0, keepdims=True)


def _project(x, g, w, rope, *, rope_cols, scale_cols=0, scale=1.0, kmean_cols=None, name):
    b, s, d = x.shape
    n_out = w.shape[1]
    rc, ra, rb = rope
    blocks_per_tile = ROW_TILE // MOBA_BLOCK
    out_shape = [jax.ShapeDtypeStruct((b, s, n_out), jnp.bfloat16)]
    out_specs = [pl.BlockSpec((1, ROW_TILE, n_out), lambda i, j: (i, j, 0))]
    if kmean_cols is not None:
        kw = kmean_cols[1] - kmean_cols[0]
        out_shape.append(jax.ShapeDtypeStruct((b * (s // ROW_TILE), blocks_per_tile, kw), jnp.float32))
        out_specs.append(pl.BlockSpec((1, blocks_per_tile, kw), lambda i, j: (i * (s // ROW_TILE) + j, 0, 0)))
    tab_spec = pl.BlockSpec((1, ROW_TILE, LANES), lambda i, j: (i, j, 0))
    outs = pl.pallas_call(
        functools.partial(_proj_kernel, n_out=n_out, rope_cols=rope_cols, scale_cols=scale_cols,
                          scale=scale, kmean_cols=kmean_cols),
        grid=(b, s // ROW_TILE),
        in_specs=[
            pl.BlockSpec((1, ROW_TILE, d), lambda i, j: (i, j, 0)),
            _const_spec((1, d)),
            _const_spec((d, n_out)),
            tab_spec, tab_spec, tab_spec,
        ],
        out_specs=out_specs,
        out_shape=out_shape,
        compiler_params=pltpu.CompilerParams(
            dimension_semantics=("parallel", "parallel"), vmem_limit_bytes=VMEM_LIMIT),
        name=name,
    )(x, g.reshape(1, d), w, rc, ra, rb)
    if kmean_cols is not None:
        return outs[0], outs[1].reshape(b, s // MOBA_BLOCK, kmean_cols[1] - kmean_cols[0])
    return outs[0]


def _moba_kernel(q_ref, k_ref, v_ref, km_ref, o_ref, vt_ref, bias_ref, s_ref, p_ref, m_ref, a_ref, acc_ref, *,
                 n_blocks, n_pairs):
    i = pl.program_id(2)
    cols = 2 * MOBA_BLOCK

    @pl.when(i == 0)
    def _():
        ones = jnp.ones((MOBA_VROWS - HEAD_DIM, MOBA_BLOCK), vt_ref.dtype)
        for c in range(n_blocks):
            sl = slice(c * MOBA_BLOCK, (c + 1) * MOBA_BLOCK)
            for pr in range(n_pairs):
                vt = v_ref[0, sl, pr * LANES:(pr + 1) * LANES].astype(jnp.float32).T.astype(vt_ref.dtype)
                for hh in range(2):
                    hd = 2 * pr + hh
                    vt_ref[hd * MOBA_VROWS:hd * MOBA_VROWS + HEAD_DIM, sl] = vt[hh * HEAD_DIM:(hh + 1) * HEAD_DIM]
                    vt_ref[hd * MOBA_VROWS + HEAD_DIM:(hd + 1) * MOBA_VROWS, sl] = ones

    lane = lax.broadcasted_iota(jnp.int32, (1, LANES), 1)
    head_lanes = (lane < HEAD_DIM, lane >= HEAD_DIM)
    blk = lax.broadcasted_iota(jnp.int32, (n_blocks, cols), 0)
    past = blk < i
    qcat = []
    gates = []
    for pr in range(n_pairs):
        q2 = q_ref[0, :, pr * LANES:(pr + 1) * LANES]
        qc = jnp.concatenate([jnp.where(head_lanes[hh], q2, jnp.zeros_like(q2)) for hh in range(2)], axis=0)
        qc = qc.astype(jnp.float32).T.astype(jnp.bfloat16)
        qcat.append(qc)
        km = km_ref[0, :, pr * LANES:(pr + 1) * LANES]
        hi = _top16(km)
        mid = _top16(km - hi)
        lo = km - hi - mid
        limbs = jnp.concatenate([hi, mid, lo], axis=0).astype(jnp.bfloat16)
        parts = jnp.dot(limbs, qc, preferred_element_type=jnp.float32)
        gates.append(parts[:n_blocks] + parts[n_blocks:2 * n_blocks] + parts[2 * n_blocks:])

    def scores(j, pr):
        start = pl.multiple_of(j * MOBA_BLOCK, MOBA_BLOCK)
        kj = k_ref[0, pl.ds(start, MOBA_BLOCK), pr * LANES:(pr + 1) * LANES]
        return jnp.dot(kj, qcat[pr], preferred_element_type=jnp.float32)

    def weighted_values(j, pr, pb):
        start = pl.multiple_of(j * MOBA_BLOCK, MOBA_BLOCK)
        outs = []
        for hh in range(2):
            r0 = (2 * pr + hh) * MOBA_VROWS
            vt = vt_ref[r0:r0 + MOBA_VROWS, pl.ds(start, MOBA_BLOCK)]
            outs.append(jnp.dot(vt, pb[:, hh * MOBA_BLOCK:(hh + 1) * MOBA_BLOCK],
                                preferred_element_type=jnp.float32))
        return jnp.concatenate(outs, axis=1)

    def stage(pr, sc):
        s_ref[pr] = sc
        return jnp.max(sc, axis=0, keepdims=True)

    key_i = lax.broadcasted_iota(jnp.int32, (MOBA_BLOCK, cols), 0)
    qry_i = lax.broadcasted_iota(jnp.int32, (MOBA_BLOCK, cols), 1) & (MOBA_BLOCK - 1)
    causal = key_i <= qry_i
    own = [scores(i, pr) for pr in range(n_pairs)]
    first_max = tuple(stage(pr, scores(0, pr)) for pr in range(n_pairs))
    for pr in range(n_pairs):
        g = jnp.where(past, gates[pr], NEG)
        rank = jnp.zeros((n_blocks, cols), jnp.int32)
        for jp in range(n_blocks):
            gj = g[jp:jp + 1, :]
            ahead = (gj > g) | ((gj == g) & (jp < blk))
            rank = rank + ahead.astype(jnp.int32)
        sel = past & (rank < MOBA_TOPK)
        bias_ref[pr] = jnp.where(sel, 0.0, NEG)
    for pr in range(n_pairs):
        s = jnp.where(causal, own[pr], NEG)
        m = jnp.max(s, axis=0, keepdims=True)
        p_ref[pr] = jnp.exp(s - m).astype(p_ref.dtype)
        m_ref[pr] = m
        a_ref[pr] = jnp.ones((1, cols), jnp.float32)
        acc_ref[pr] = jnp.zeros((MOBA_VROWS, cols), jnp.float32)

    def body(t, cur_max):
        cur = [s_ref[pr] for pr in range(n_pairs)]
        p_prev = [p_ref[pr] for pr in range(n_pairs)]
        state = [(m_ref[pr], a_ref[pr], acc_ref[pr]) for pr in range(n_pairs)]
        row = [bias_ref[pr, pl.ds(t - 1, 1), :] for pr in range(n_pairs)]
        prev_blk = jnp.where(t == 1, i, t - 2)
        nxt = [scores(t, pr) for pr in range(n_pairs)]
        pv = [weighted_values(prev_blk, pr, p_prev[pr]) for pr in range(n_pairs)]
        nxt_max = []
        for pr in range(n_pairs):
            m_old, a_prev, acc_old = state[pr]
            m_new = jnp.maximum(m_old, cur_max[pr] + row[pr])
            p_ref[pr] = jnp.exp(cur[pr] - (m_new - row[pr])).astype(p_ref.dtype)
            a_ref[pr] = jnp.exp(m_old - m_new)
            m_ref[pr] = m_new
            acc_ref[pr] = acc_old * a_prev + pv[pr]
            nxt_max.append(stage(pr, nxt[pr]))
        return tuple(nxt_max)

    lax.fori_loop(1, i + 1, body, first_max)

    last_blk = jnp.maximum(i - 1, 0)
    for pr in range(n_pairs):
        acc = acc_ref[pr] * a_ref[pr] + weighted_values(last_blk, pr, p_ref[pr])
        out = acc[:HEAD_DIM] / acc[HEAD_DIM:HEAD_DIM + 1]
        out_t = jnp.concatenate([out[:, :MOBA_BLOCK], out[:, MOBA_BLOCK:]], axis=0)
        o_ref[0, :, pr * LANES:(pr + 1) * LANES] = out_t.T.astype(o_ref.dtype)


def _moba_attention(qkv, kmean):
    b, s, _ = qkv.shape
    n_blocks = s // MOBA_BLOCK
    width = MOBA_PAIRS * LANES
    groups = N_HEADS * HEAD_DIM // width
    cols = 2 * MOBA_BLOCK
    return pl.pallas_call(
        functools.partial(_moba_kernel, n_blocks=n_blocks, n_pairs=MOBA_PAIRS),
        grid=(b, groups, n_blocks),
        in_specs=[
            pl.BlockSpec((1, MOBA_BLOCK, width), lambda bi, g, i: (bi, i, g)),
            pl.BlockSpec((1, s, width), lambda bi, g, i: (bi, 0, groups + g)),
            pl.BlockSpec((1, s, width), lambda bi, g, i: (bi, 0, 2 * groups + g)),
            pl.BlockSpec((1, n_blocks, width), lambda bi, g, i: (bi, 0, g)),
        ],
        out_specs=pl.BlockSpec((1, MOBA_BLOCK, width), lambda bi, g, i: (bi, i, g)),
        out_shape=jax.ShapeDtypeStruct((b, s, N_HEADS * HEAD_DIM), jnp.bfloat16),
        scratch_shapes=[
            pltpu.VMEM((2 * MOBA_PAIRS * MOBA_VROWS, s), jnp.bfloat16),
            pltpu.VMEM((MOBA_PAIRS, n_blocks, cols), jnp.float32),
            pltpu.VMEM((MOBA_PAIRS, MOBA_BLOCK, cols), jnp.float32),
            pltpu.VMEM((MOBA_PAIRS, MOBA_BLOCK, cols), jnp.bfloat16),
            pltpu.VMEM((MOBA_PAIRS, 1, cols), jnp.float32),
            pltpu.VMEM((MOBA_PAIRS, 1, cols), jnp.float32),
            pltpu.VMEM((MOBA_PAIRS, MOBA_VROWS, cols), jnp.float32),
        ],
        compiler_params=pltpu.CompilerParams(
            dimension_semantics=("parallel", "parallel", "arbitrary"), vmem_limit_bytes=VMEM_LIMIT),
        name="moba_attention",
    )(qkv, qkv, qkv, kmean)


def _swa_kernel(sink_ref, q_ref, k_ref, v_ref, o_ref):
    a = pl.program_id(1)
    t = pl.program_id(2)
    lane = lax.broadcasted_iota(jnp.int32, (1, LANES), 1)
    half_lanes = (lane < HEAD_DIM, lane >= HEAD_DIM)
    r = lax.broadcasted_iota(jnp.int32, (WINDOW, 2 * WINDOW), 0)
    c = lax.broadcasted_iota(jnp.int32, (WINDOW, 2 * WINDOW), 1)
    band = (c > r) & (c <= r + WINDOW)
    own_only = c >= WINDOW
    for w in range(SWA_Q_TILE // WINDOW):
        n = t * (SWA_Q_TILE // WINDOW) + w
        own0 = pl.multiple_of(n * WINDOW, WINDOW)
        prev0 = pl.multiple_of(jnp.maximum(n - 1, 0) * WINDOW, WINDOW)
        kd = jnp.concatenate([k_ref[0, pl.ds(prev0, WINDOW), :], k_ref[0, pl.ds(own0, WINDOW), :]], axis=0)
        vd = jnp.concatenate([v_ref[0, pl.ds(prev0, WINDOW), :], v_ref[0, pl.ds(own0, WINDOW), :]], axis=0)
        k_half = [jnp.where(half_lanes[e], kd, jnp.zeros_like(kd)) for e in range(2)]
        v_half = [jnp.where(half_lanes[e], vd, jnp.zeros_like(vd)) for e in range(2)]
        mask = band & (own_only | (n > 0))
        for u in range(GROUP * HEAD_DIM // LANES):
            q2 = q_ref[0, w * WINDOW:(w + 1) * WINDOW, u * LANES:(u + 1) * LANES]
            out = None
            for e in range(2):
                sink = sink_ref[a * GROUP + 2 * u + e]
                s = lax.dot_general(q2, k_half[e], _NT, preferred_element_type=jnp.float32)
                s = jnp.where(mask, s, NEG)
                m = jnp.maximum(jnp.max(s, axis=1, keepdims=True), sink)
                p = jnp.exp(s - m)
                denom = jnp.sum(p, axis=1, keepdims=True) + jnp.exp(sink - m)
                pv = jnp.dot(p.astype(jnp.bfloat16), v_half[e], preferred_element_type=jnp.float32) / denom
                out = pv if out is None else out + pv
            o_ref[0, w * WINDOW:(w + 1) * WINDOW, u * LANES:(u + 1) * LANES] = out.astype(o_ref.dtype)


def _swa_attention(q, kv, sinks):
    b, s, _ = q.shape
    gw = GROUP * HEAD_DIM
    grid_spec = pltpu.PrefetchScalarGridSpec(
        num_scalar_prefetch=1,
        grid=(b, N_KV_HEADS, s // SWA_Q_TILE),
        in_specs=[
            pl.BlockSpec((1, SWA_Q_TILE, gw), lambda bi, a, t, sk: (bi, t, a)),
            pl.BlockSpec((1, s, LANES), lambda bi, a, t, sk: (bi, 0, a)),
            pl.BlockSpec((1, s, LANES), lambda bi, a, t, sk: (bi, 0, N_KV_HEADS + a)),
        ],
        out_specs=pl.BlockSpec((1, SWA_Q_TILE, gw), lambda bi, a, t, sk: (bi, t, a)),
    )
    return pl.pallas_call(
        _swa_kernel,
        grid_spec=grid_spec,
        out_shape=jax.ShapeDtypeStruct((b, s, N_HEADS * HEAD_DIM), jnp.bfloat16),
        compiler_params=pltpu.CompilerParams(
            dimension_semantics=("parallel", "parallel", "parallel"), vmem_limit_bytes=VMEM_LIMIT),
        name="swa_attention",
    )(sinks, q, kv, kv)


def _ffn_kernel(x_ref, o_ref, wo_ref, g_ref, wu_ref, cw_ref, cb_ref, wd_ref, gf_ref, out_ref,
                h_ref, act_ref, carry_ref, *, final):
    ts = x_ref.shape[1]

    @pl.when(pl.program_id(1) == 0)
    def _():
        carry_ref[...] = jnp.zeros_like(carry_ref)

    x1 = x_ref[0] + jnp.dot(o_ref[0], wo_ref[...], preferred_element_type=jnp.float32)
    out_ref[0] = x1
    h_ref[...] = _rms(x1, g_ref[...]).astype(jnp.bfloat16)
    row = lax.broadcasted_iota(jnp.int32, (ts, 1), 0)

    def conv(u, tail, taps, bias):
        last1 = tail[7:8]
        last2 = tail[6:7]
        u1 = jnp.where(row == 0, last1, pltpu.roll(u, 1, axis=0))
        u2 = jnp.where(row == 0, last2, jnp.where(row == 1, last1, pltpu.roll(u, 2, axis=0)))
        return bias + taps[0:1] * u2 + taps[1:2] * u1 + taps[2:3] * u

    def cols(c, half):
        start = half * D_FF + c * FF_CHUNK
        return slice(start, start + FF_CHUNK)

    def up(c):
        h = h_ref[...]
        return tuple(jnp.dot(h, wu_ref[:, cols(c, half)], preferred_element_type=jnp.float32)
                     for half in range(2))

    def mix(c, ug, uv):
        gate = conv(ug, carry_ref[c, 0], cw_ref[:, cols(c, 0)], cb_ref[:, cols(c, 0)])
        val = conv(uv, carry_ref[c, 1], cw_ref[:, cols(c, 1)], cb_ref[:, cols(c, 1)])
        carry_ref[c, 0] = ug[ts - 8:]
        carry_ref[c, 1] = uv[ts - 8:]
        return (gate / (1.0 + jnp.exp(-gate)) * val).astype(jnp.bfloat16)

    nxt = up(0)
    for c in range(N_FF_CHUNKS):
        cur = nxt
        if c + 1 < N_FF_CHUNKS:
            nxt = up(c + 1)
        act_ref[:, c * FF_CHUNK:(c + 1) * FF_CHUNK] = mix(c, *cur)
    y = out_ref[0] + jnp.dot(act_ref[...], wd_ref[...], preferred_element_type=jnp.float32)
    out_ref[0] = _rms(y, gf_ref[...]) if final else y


def _attn_out_ffn(x, o, wo, g, wu, cw, cb, wd, gf, *, final, name):
    b, s, d = x.shape
    row_spec = pl.BlockSpec((1, ROW_TILE, d), lambda i, j: (i, j, 0))
    return pl.pallas_call(
        functools.partial(_ffn_kernel, final=final),
        grid=(b, s // ROW_TILE),
        in_specs=[
            row_spec, row_spec,
            _const_spec((d, d)),
            _const_spec((1, d)),
            _const_spec((d, 2 * D_FF)),
            _const_spec((CONV_W, 2 * D_FF)),
            _const_spec((1, 2 * D_FF)),
            _const_spec((D_FF, d)),
            _const_spec((1, d)),
        ],
        out_specs=row_spec,
        out_shape=jax.ShapeDtypeStruct((b, s, d), jnp.float32),
        scratch_shapes=[
            pltpu.VMEM((ROW_TILE, d), jnp.bfloat16),
            pltpu.VMEM((ROW_TILE, D_FF), jnp.bfloat16),
            pltpu.VMEM((N_FF_CHUNKS, 2, 8, FF_CHUNK), jnp.float32),
        ],
        compiler_params=pltpu.CompilerParams(
            dimension_semantics=("parallel", "arbitrary"), vmem_limit_bytes=VMEM_LIMIT),
        name=name,
    )(x, o, wo, g.reshape(1, d), wu, cw, cb.reshape(1, 2 * D_FF), wd, gf.reshape(1, d))


def _rope_tables(positions):
    inv_freq = ROPE_THETA ** (-jnp.arange(0, ROT_DIM, 2, dtype=jnp.float32) / ROT_DIM)
    ang = positions.astype(jnp.float32)[..., None] * inv_freq
    cos, sin = jnp.cos(ang), jnp.sin(ang)
    half = ROT_DIM // 2
    rest = HEAD_DIM - ROT_DIM
    shape = ang.shape[:-1]
    ones = jnp.ones(shape + (rest,), jnp.float32)
    z_half = jnp.zeros(shape + (half,), jnp.float32)
    z_rest = jnp.zeros(shape + (rest,), jnp.float32)
    rc = jnp.concatenate([cos, cos, ones], axis=-1)
    ra = jnp.concatenate([-sin, z_half, z_rest], axis=-1)
    rb = jnp.concatenate([z_half, sin, z_rest], axis=-1)
    reps = LANES // HEAD_DIM
    return tuple(jnp.tile(t, (1, 1, reps)) for t in (rc, ra, rb))


def _dup_heads(w):
    d = w.shape[0]
    w = w.reshape(d, N_KV_HEADS, 1, HEAD_DIM)
    return jnp.broadcast_to(w, (d, N_KV_HEADS, LANES // HEAD_DIM, HEAD_DIM)).reshape(d, N_KV_HEADS * LANES)


def kernel(x, positions, attn_norm, w_qkv_a, w_o_a, kv_norm, w_kv_b, w_q_b, sinks_b, w_o_b, ffn_norm,
           w_up, conv_w, conv_b, w_down, final_norm):
    assert x.shape[1] % ROW_TILE == 0 and x.shape[1] % SWA_Q_TILE == 0 and x.shape[2] == D_MODEL
    rope = _rope_tables(positions)
    scale = HEAD_DIM ** -0.5
    qw = N_HEADS * HEAD_DIM
    kvw = N_KV_HEADS * HEAD_DIM
    kv = None
    for l in range(DEPTH):
        if l < N_A:
            qkv, kmean = _project(
                x, attn_norm[l], w_qkv_a[l].astype(jnp.bfloat16), rope, rope_cols=2 * qw, scale_cols=qw,
                scale=scale, kmean_cols=(qw, 2 * qw), name=f"proj_qkv_{l}")
            o = _moba_attention(qkv, kmean)
            wo = w_o_a[l]
        else:
            i = l - N_A
            if l == N_A:
                w_kv = jnp.concatenate([_dup_heads(w_kv_b[:, :kvw]), _dup_heads(w_kv_b[:, kvw:])], axis=1)
                kv = _project(x, kv_norm, w_kv.astype(jnp.bfloat16), rope, rope_cols=N_KV_HEADS * LANES,
                              name="proj_kv")
            q = _project(x, attn_norm[l], w_q_b[i].astype(jnp.bfloat16), rope, rope_cols=qw, scale_cols=qw,
                         scale=scale, name=f"proj_q_{l}")
            o = _swa_attention(q, kv, sinks_b[i])
            wo = w_o_b[i]
        x = _attn_out_ffn(x, o, wo.astype(jnp.bfloat16), ffn_norm[l], w_up[l].astype(jnp.bfloat16), conv_w[l],
                          conv_b[l], w_down[l].astype(jnp.bfloat16), final_norm,
                          final=(l == DEPTH - 1), name=f"attn_out_ffn_{l}")
    return x
```

```python
import functools

import jax
import jax.numpy as jnp
from jax import lax
from jax.experimental import pallas as pl
from jax.experimental.pallas import tpu as pltpu

D_MODEL = 1024
DEPTH = 4
N_HEADS = 16
HEAD_DIM = 64
N_KV_HEADS = 4
GROUP = N_HEADS // N_KV_HEADS
ROT_DIM = HEAD_DIM // 4
ROPE_THETA = 500000.0
MOBA_BLOCK = 256
MOBA_TOPK = 3
WINDOW = 128
D_FF = 2816
CONV_W = 3
N_A = DEPTH // 2
EPS = 1e-6
NEG = -1e30
LOWEST_F32 = -3.4028234663852886e38

LANES = 128
ROW_TILE = 512
PROJ_COL_CHUNK = 256
FF_CHUNK = 256
N_FF_CHUNKS = D_FF // FF_CHUNK
SWA_Q_TILE = 1024
MOBA_PAIRS = 4
MOBA_VROWS = HEAD_DIM + 16
VMEM_LIMIT = 56 * 1024 * 1024

_NT = (((1,), (1,)), ((), ()))


def _rms(x, g):
    ms = jnp.mean(x * x, axis=-1, keepdims=True)
    return x * lax.rsqrt(ms + EPS) * g


def _top16(x):
    bits = lax.bitcast_convert_type(x, jnp.int32) & jnp.int32(-65536)
    return lax.bitcast_convert_type(bits, jnp.float32)


def _const_spec(shape):
    zeros = (0,) * len(shape)
    return pl.BlockSpec(shape, lambda *_: zeros, pipeline_mode=pl.Buffered(1))


def _proj_kernel(x_ref, g_ref, w_ref, rc_ref, ra_ref, rb_ref, *out_refs,
                 n_out, rope_cols, scale_cols, scale, kmean_cols):
    y_ref = out_refs[0]
    h = _rms(x_ref[0], g_ref[...]).astype(jnp.bfloat16)
    rc = rc_ref[0]
    ra = ra_ref[0]
    rb = rb_ref[0]
    ts = h.shape[0]
    for c0 in range(0, n_out, PROJ_COL_CHUNK):
        width = min(PROJ_COL_CHUNK, n_out - c0)
        y = jnp.dot(h, w_ref[:, c0:c0 + width], preferred_element_type=jnp.float32)
        for l0 in range(0, width, LANES):
            col = c0 + l0
            t = y[:, l0:l0 + LANES]
            if col < rope_cols:
                t = (t * rc + pltpu.roll(t, LANES - ROT_DIM // 2, axis=1) * ra
                     + pltpu.roll(t, ROT_DIM // 2, axis=1) * rb)
            if col < scale_cols:
                t = t * scale
            y_ref[0, :, col:col + LANES] = t.astype(y_ref.dtype)
            if kmean_cols is not None and kmean_cols[0] <= col < kmean_cols[1]:
                km_ref = out_refs[1]
                kc = col - kmean_cols[0]
                for blk in range(ts // MOBA_BLOCK):
                    seg = t[blk * MOBA_BLOCK:(blk + 1) * MOBA_BLOCK]
                    km_ref[0, blk:blk + 1, kc:kc + LANES] = jnp.mean(seg, axis=0, keepdims=True)


def _project(x, g, w, rope, *, rope_cols, scale_cols=0, scale=1.0, kmean_cols=None, name):
    b, s, d = x.shape
    n_out = w.shape[1]
    rc, ra, rb = rope
    blocks_per_tile = ROW_TILE // MOBA_BLOCK
    out_shape = [jax.ShapeDtypeStruct((b, s, n_out), jnp.bfloat16)]
    out_specs = [pl.BlockSpec((1, ROW_TILE, n_out), lambda i, j: (i, j, 0))]
    if kmean_cols is not None:
        kw = kmean_cols[1] - kmean_cols[0]
        out_shape.append(jax.ShapeDtypeStruct((b * (s // ROW_TILE), blocks_per_tile, kw), jnp.float32))
        out_specs.append(pl.BlockSpec((1, blocks_per_tile, kw), lambda i, j: (i * (s // ROW_TILE) + j, 0, 0)))
    tab_spec = pl.BlockSpec((1, ROW_TILE, LANES), lambda i, j: (i, j, 0))
    outs = pl.pallas_call(
        functools.partial(_proj_kernel, n_out=n_out, rope_cols=rope_cols, scale_cols=scale_cols,
                          scale=scale, kmean_cols=kmean_cols),
        grid=(b, s // ROW_TILE),
        in_specs=[
            pl.BlockSpec((1, ROW_TILE, d), lambda i, j: (i, j, 0)),
            _const_spec((1, d)),
            _const_spec((d, n_out)),
            tab_spec, tab_spec, tab_spec,
        ],
        out_specs=out_specs,
        out_shape=out_shape,
        compiler_params=pltpu.CompilerParams(
            dimension_semantics=("parallel", "parallel"), vmem_limit_bytes=VMEM_LIMIT),
        name=name,
    )(x, g.reshape(1, d), w, rc, ra, rb)
    if kmean_cols is not None:
        return outs[0], outs[1].reshape(b, s // MOBA_BLOCK, kmean_cols[1] - kmean_cols[0])
    return outs[0]


def _moba_kernel(q_ref, k_ref, v_ref, km_ref, o_ref, vt_ref, bias_ref, s_ref, p_ref, m_ref, a_ref, acc_ref, *,
                 n_blocks, n_pairs):
    i = pl.program_id(2)
    cols = 2 * MOBA_BLOCK

    @pl.when(i == 0)
    def _():
        ones = jnp.ones((MOBA_VROWS - HEAD_DIM, MOBA_BLOCK), vt_ref.dtype)
        for c in range(n_blocks):
            sl = slice(c * MOBA_BLOCK, (c + 1) * MOBA_BLOCK)
            for pr in range(n_pairs):
                vt = v_ref[0, sl, pr * LANES:(pr + 1) * LANES].astype(jnp.float32).T.astype(vt_ref.dtype)
                for hh in range(2):
                    hd = 2 * pr + hh
                    vt_ref[hd * MOBA_VROWS:hd * MOBA_VROWS + HEAD_DIM, sl] = vt[hh * HEAD_DIM:(hh + 1) * HEAD_DIM]
                    vt_ref[hd * MOBA_VROWS + HEAD_DIM:(hd + 1) * MOBA_VROWS, sl] = ones

    lane = lax.broadcasted_iota(jnp.int32, (1, LANES), 1)
    head_lanes = (lane < HEAD_DIM, lane >= HEAD_DIM)
    blk = lax.broadcasted_iota(jnp.int32, (n_blocks, cols), 0)
    past = blk < i
    qcat = []
    gates = []
    for pr in range(n_pairs):
        q2 = q_ref[0, :, pr * LANES:(pr + 1) * LANES]
        qc = jnp.concatenate([jnp.where(head_lanes[hh], q2, jnp.zeros_like(q2)) for hh in range(2)], axis=0)
        qc = qc.astype(jnp.float32).T.astype(jnp.bfloat16)
        qcat.append(qc)
        km = km_ref[0, :, pr * LANES:(pr + 1) * LANES]
        hi = _top16(km)
        mid = _top16(km - hi)
        lo = km - hi - mid
        limbs = jnp.concatenate([hi, mid, lo], axis=0).astype(jnp.bfloat16)
        parts = jnp.dot(limbs, qc, preferred_element_type=jnp.float32)
        gates.append(parts[:n_blocks] + parts[n_blocks:2 * n_blocks] + parts[2 * n_blocks:])

    def scores(j, pr):
        start = pl.multiple_of(j * MOBA_BLOCK, MOBA_BLOCK)
        kj = k_ref[0, pl.ds(start, MOBA_BLOCK), pr * LANES:(pr + 1) * LANES]
        return jnp.dot(kj, qcat[pr], preferred_element_type=jnp.float32)

    def weighted_values(j, pr, pb):
        start = pl.multiple_of(j * MOBA_BLOCK, MOBA_BLOCK)
        outs = []
        for hh in range(2):
            r0 = (2 * pr + hh) * MOBA_VROWS
            vt = vt_ref[r0:r0 + MOBA_VROWS, pl.ds(start, MOBA_BLOCK)]
            outs.append(jnp.dot(vt, pb[:, hh * MOBA_BLOCK:(hh + 1) * MOBA_BLOCK],
                                preferred_element_type=jnp.float32))
        return jnp.concatenate(outs, axis=1)

    def rescaled(acc, a):
        return jnp.maximum(acc * a, LOWEST_F32)

    key_i = lax.broadcasted_iota(jnp.int32, (MOBA_BLOCK, cols), 0)
    qry_i = lax.broadcasted_iota(jnp.int32, (MOBA_BLOCK, cols), 1) & (MOBA_BLOCK - 1)
    causal = key_i <= qry_i
    own = [scores(i, pr) for pr in range(n_pairs)]
    for pr in range(n_pairs):
        s_ref[pr] = scores(0, pr)
    for pr in range(n_pairs):
        g = jnp.where(past, gates[pr], NEG)
        rank = jnp.zeros((n_blocks, cols), jnp.int32)
        for jp in range(n_blocks):
            gj = g[jp:jp + 1, :]
            ahead = (gj > g) | ((gj == g) & (jp < blk))
            rank = rank + ahead.astype(jnp.int32)
        sel = past & (rank < MOBA_TOPK)
        bias_ref[pr] = jnp.where(sel, 0.0, NEG)
    for pr in range(n_pairs):
        s = jnp.where(causal, own[pr], NEG)
        m = jnp.max(s, axis=0, keepdims=True)
        p_ref[pr] = jnp.exp(s - m).astype(p_ref.dtype)
        m_ref[pr] = m
        a_ref[pr] = jnp.ones((1, cols), jnp.float32)
        acc_ref[pr] = jnp.zeros((MOBA_VROWS, cols), jnp.float32)

    def body(t, carry):
        cur = [s_ref[pr] for pr in range(n_pairs)]
        p_prev = [p_ref[pr] for pr in range(n_pairs)]
        state = [(m_ref[pr], a_ref[pr], acc_ref[pr]) for pr in range(n_pairs)]
        row = [bias_ref[pr, pl.ds(t - 1, 1), :] for pr in range(n_pairs)]
        prev_blk = jnp.where(t == 1, i, t - 2)
        nxt = [scores(t, pr) for pr in range(n_pairs)]
        pv = [weighted_values(prev_blk, pr, p_prev[pr]) for pr in range(n_pairs)]
        for pr in range(n_pairs):
            m_old, a_prev, acc_old = state[pr]
            m_new = jnp.maximum(m_old, jnp.max(cur[pr], axis=0, keepdims=True) + row[pr])
            p_ref[pr] = jnp.exp(cur[pr] - (m_new - row[pr])).astype(p_ref.dtype)
            a_ref[pr] = jnp.exp(m_old - m_new)
            m_ref[pr] = m_new
            acc_ref[pr] = rescaled(acc_old, a_prev) + pv[pr]
            s_ref[pr] = nxt[pr]
        return carry

    lax.fori_loop(1, i + 1, body, 0)

    last_blk = jnp.maximum(i - 1, 0)
    for pr in range(n_pairs):
        acc = rescaled(acc_ref[pr], a_ref[pr]) + weighted_values(last_blk, pr, p_ref[pr])
        out = acc[:HEAD_DIM] / acc[HEAD_DIM:HEAD_DIM + 1]
        out_t = jnp.concatenate([out[:, :MOBA_BLOCK], out[:, MOBA_BLOCK:]], axis=0)
        o_ref[0, :, pr * LANES:(pr + 1) * LANES] = out_t.T.astype(o_ref.dtype)


def _moba_attention(qkv, kmean):
    b, s, _ = qkv.shape
    n_blocks = s // MOBA_BLOCK
    width = MOBA_PAIRS * LANES
    groups = N_HEADS * HEAD_DIM // width
    cols = 2 * MOBA_BLOCK
    return pl.pallas_call(
        functools.partial(_moba_kernel, n_blocks=n_blocks, n_pairs=MOBA_PAIRS),
        grid=(b, groups, n_blocks),
        in_specs=[
            pl.BlockSpec((1, MOBA_BLOCK, width), lambda bi, g, i: (bi, i, g)),
            pl.BlockSpec((1, s, width), lambda bi, g, i: (bi, 0, groups + g)),
            pl.BlockSpec((1, s, width), lambda bi, g, i: (bi, 0, 2 * groups + g)),
            pl.BlockSpec((1, n_blocks, width), lambda bi, g, i: (bi, 0, g)),
        ],
        out_specs=pl.BlockSpec((1, MOBA_BLOCK, width), lambda bi, g, i: (bi, i, g)),
        out_shape=jax.ShapeDtypeStruct((b, s, N_HEADS * HEAD_DIM), jnp.bfloat16),
        scratch_shapes=[
            pltpu.VMEM((2 * MOBA_PAIRS * MOBA_VROWS, s), jnp.bfloat16),
            pltpu.VMEM((MOBA_PAIRS, n_blocks, cols), jnp.float32),
            pltpu.VMEM((MOBA_PAIRS, MOBA_BLOCK, cols), jnp.float32),
            pltpu.VMEM((MOBA_PAIRS, MOBA_BLOCK, cols), jnp.bfloat16),
            pltpu.VMEM((MOBA_PAIRS, 1, cols), jnp.float32),
            pltpu.VMEM((MOBA_PAIRS, 1, cols), jnp.float32),
            pltpu.VMEM((MOBA_PAIRS, MOBA_VROWS, cols), jnp.float32),
        ],
        compiler_params=pltpu.CompilerParams(
            dimension_semantics=("parallel", "parallel", "arbitrary"), vmem_limit_bytes=VMEM_LIMIT),
        name="moba_attention",
    )(qkv, qkv, qkv, kmean)


def _swa_kernel(sink_ref, q_ref, k_ref, v_ref, o_ref):
    a = pl.program_id(1)
    t = pl.program_id(2)
    lane = lax.broadcasted_iota(jnp.int32, (1, LANES), 1)
    half_lanes = (lane < HEAD_DIM, lane >= HEAD_DIM)
    r = lax.broadcasted_iota(jnp.int32, (WINDOW, 2 * WINDOW), 0)
    c = lax.broadcasted_iota(jnp.int32, (WINDOW, 2 * WINDOW), 1)
    band = (c > r) & (c <= r + WINDOW)
    own_only = c >= WINDOW
    for w in range(SWA_Q_TILE // WINDOW):
        n = t * (SWA_Q_TILE // WINDOW) + w
        own0 = pl.multiple_of(n * WINDOW, WINDOW)
        prev0 = pl.multiple_of(jnp.maximum(n - 1, 0) * WINDOW, WINDOW)
        kd = jnp.concatenate([k_ref[0, pl.ds(prev0, WINDOW), :], k_ref[0, pl.ds(own0, WINDOW), :]], axis=0)
        vd = jnp.concatenate([v_ref[0, pl.ds(prev0, WINDOW), :], v_ref[0, pl.ds(own0, WINDOW), :]], axis=0)
        k_half = [jnp.where(half_lanes[e], kd, jnp.zeros_like(kd)) for e in range(2)]
        v_half = [jnp.where(half_lanes[e], vd, jnp.zeros_like(vd)) for e in range(2)]
        mask = band & (own_only | (n > 0))
        for u in range(GROUP * HEAD_DIM // LANES):
            q2 = q_ref[0, w * WINDOW:(w + 1) * WINDOW, u * LANES:(u + 1) * LANES]
            out = None
            for e in range(2):
                sink = sink_ref[a * GROUP + 2 * u + e]
                s = lax.dot_general(q2, k_half[e], _NT, preferred_element_type=jnp.float32)
                s = jnp.where(mask, s, NEG)
                m = jnp.maximum(jnp.max(s, axis=1, keepdims=True), sink)
                p = jnp.exp(s - m)
                denom = jnp.sum(p, axis=1, keepdims=True) + jnp.exp(sink - m)
                pv = jnp.dot(p.astype(jnp.bfloat16), v_half[e], preferred_element_type=jnp.float32) / denom
                out = pv if out is None else out + pv
            o_ref[0, w * WINDOW:(w + 1) * WINDOW, u * LANES:(u + 1) * LANES] = out.astype(o_ref.dtype)


def _swa_attention(q, kv, sinks):
    b, s, _ = q.shape
    gw = GROUP * HEAD_DIM
    grid_spec = pltpu.PrefetchScalarGridSpec(
        num_scalar_prefetch=1,
        grid=(b, N_KV_HEADS, s // SWA_Q_TILE),
        in_specs=[
            pl.BlockSpec((1, SWA_Q_TILE, gw), lambda bi, a, t, sk: (bi, t, a)),
            pl.BlockSpec((1, s, LANES), lambda bi, a, t, sk: (bi, 0, a)),
            pl.BlockSpec((1, s, LANES), lambda bi, a, t, sk: (bi, 0, N_KV_HEADS + a)),
        ],
        out_specs=pl.BlockSpec((1, SWA_Q_TILE, gw), lambda bi, a, t, sk: (bi, t, a)),
    )
    return pl.pallas_call(
        _swa_kernel,
        grid_spec=grid_spec,
        out_shape=jax.ShapeDtypeStruct((b, s, N_HEADS * HEAD_DIM), jnp.bfloat16),
        compiler_params=pltpu.CompilerParams(
            dimension_semantics=("parallel", "parallel", "parallel"), vmem_limit_bytes=VMEM_LIMIT),
        name="swa_attention",
    )(sinks, q, kv, kv)


def _ffn_kernel(x_ref, o_ref, wo_ref, g_ref, wu_ref, cw_ref, cb_ref, wd_ref, gf_ref, out_ref,
                h_ref, act_ref, carry_ref, *, final):
    ts = x_ref.shape[1]

    @pl.when(pl.program_id(1) == 0)
    def _():
        carry_ref[...] = jnp.zeros_like(carry_ref)

    x1 = x_ref[0] + jnp.dot(o_ref[0], wo_ref[...], preferred_element_type=jnp.float32)
    out_ref[0] = x1
    h_ref[...] = _rms(x1, g_ref[...]).astype(jnp.bfloat16)
    row = lax.broadcasted_iota(jnp.int32, (ts, 1), 0)

    def conv(u, tail, taps, bias):
        last1 = tail[7:8]
        last2 = tail[6:7]
        u1 = jnp.where(row == 0, last1, pltpu.roll(u, 1, axis=0))
        u2 = jnp.where(row == 0, last2, jnp.where(row == 1, last1, pltpu.roll(u, 2, axis=0)))
        return bias + taps[0:1] * u2 + taps[1:2] * u1 + taps[2:3] * u

    def cols(c, half):
        start = half * D_FF + c * FF_CHUNK
        return slice(start, start + FF_CHUNK)

    def up(c):
        h = h_ref[...]
        return tuple(jnp.dot(h, wu_ref[:, cols(c, half)], preferred_element_type=jnp.float32)
                     for half in range(2))

    def mix(c, ug, uv):
        gate = conv(ug, carry_ref[c, 0], cw_ref[:, cols(c, 0)], cb_ref[:, cols(c, 0)])
        val = conv(uv, carry_ref[c, 1], cw_ref[:, cols(c, 1)], cb_ref[:, cols(c, 1)])
        carry_ref[c, 0] = ug[ts - 8:]
        carry_ref[c, 1] = uv[ts - 8:]
        return (gate / (1.0 + jnp.exp(-gate)) * val).astype(jnp.bfloat16)

    nxt = up(0)
    for c in range(N_FF_CHUNKS):
        cur = nxt
        if c + 1 < N_FF_CHUNKS:
            nxt = up(c + 1)
        act_ref[:, c * FF_CHUNK:(c + 1) * FF_CHUNK] = mix(c, *cur)
    y = out_ref[0] + jnp.dot(act_ref[...], wd_ref[...], preferred_element_type=jnp.float32)
    out_ref[0] = _rms(y, gf_ref[...]) if final else y


def _attn_out_ffn(x, o, wo, g, wu, cw, cb, wd, gf, *, final, name):
    b, s, d = x.shape
    row_spec = pl.BlockSpec((1, ROW_TILE, d), lambda i, j: (i, j, 0))
    return pl.pallas_call(
        functools.partial(_ffn_kernel, final=final),
        grid=(b, s // ROW_TILE),
        in_specs=[
            row_spec, row_spec,
            _const_spec((d, d)),
            _const_spec((1, d)),
            _const_spec((d, 2 * D_FF)),
            _const_spec((CONV_W, 2 * D_FF)),
            _const_spec((1, 2 * D_FF)),
            _const_spec((D_FF, d)),
            _const_spec((1, d)),
        ],
        out_specs=row_spec,
        out_shape=jax.ShapeDtypeStruct((b, s, d), jnp.float32),
        scratch_shapes=[
            pltpu.VMEM((ROW_TILE, d), jnp.bfloat16),
            pltpu.VMEM((ROW_TILE, D_FF), jnp.bfloat16),
            pltpu.VMEM((N_FF_CHUNKS, 2, 8, FF_CHUNK), jnp.float32),
        ],
        compiler_params=pltpu.CompilerParams(
            dimension_semantics=("parallel", "arbitrary"), vmem_limit_bytes=VMEM_LIMIT),
        name=name,
    )(x, o, wo, g.reshape(1, d), wu, cw, cb.reshape(1, 2 * D_FF), wd, gf.reshape(1, d))


def _rope_tables(positions):
    inv_freq = ROPE_THETA ** (-jnp.arange(0, ROT_DIM, 2, dtype=jnp.float32) / ROT_DIM)
    ang = positions.astype(jnp.float32)[..., None] * inv_freq
    cos, sin = jnp.cos(ang), jnp.sin(ang)
    half = ROT_DIM // 2
    rest = HEAD_DIM - ROT_DIM
    shape = ang.shape[:-1]
    ones = jnp.ones(shape + (rest,), jnp.float32)
    z_half = jnp.zeros(shape + (half,), jnp.float32)
    z_rest = jnp.zeros(shape + (rest,), jnp.float32)
    rc = jnp.concatenate([cos, cos, ones], axis=-1)
    ra = jnp.concatenate([-sin, z_half, z_rest], axis=-1)
    rb = jnp.concatenate([z_half, sin, z_rest], axis=-1)
    reps = LANES // HEAD_DIM
    return tuple(jnp.tile(t, (1, 1, reps)) for t in (rc, ra, rb))


def _dup_heads(w):
    d = w.shape[0]
    w = w.reshape(d, N_KV_HEADS, 1, HEAD_DIM)
    return jnp.broadcast_to(w, (d, N_KV_HEADS, LANES // HEAD_DIM, HEAD_DIM)).reshape(d, N_KV_HEADS * LANES)


def kernel(x, positions, attn_norm, w_qkv_a, w_o_a, kv_norm, w_kv_b, w_q_b, sinks_b, w_o_b, ffn_norm,
           w_up, conv_w, conv_b, w_down, final_norm):
    assert x.shape[1] % ROW_TILE == 0 and x.shape[1] % SWA_Q_TILE == 0 and x.shape[2] == D_MODEL
    rope = _rope_tables(positions)
    scale = HEAD_DIM ** -0.5
    qw = N_HEADS * HEAD_DIM
    kvw = N_KV_HEADS * HEAD_DIM
    kv = None
    for l in range(DEPTH):
        if l < N_A:
            qkv, kmean = _project(
                x, attn_norm[l], w_qkv_a[l].astype(jnp.bfloat16), rope, rope_cols=2 * qw, scale_cols=qw,
                scale=scale, kmean_cols=(qw, 2 * qw), name=f"proj_qkv_{l}")
            o = _moba_attention(qkv, kmean)
            wo = w_o_a[l]
        else:
            i = l - N_A
            if l == N_A:
                w_kv = jnp.concatenate([_dup_heads(w_kv_b[:, :kvw]), _dup_heads(w_kv_b[:, kvw:])], axis=1)
                kv = _project(x, kv_norm, w_kv.astype(jnp.bfloat16), rope, rope_cols=N_KV_HEADS * LANES,
                              name="proj_kv")
            q = _project(x, attn_norm[l], w_q_b[i].astype(jnp.bfloat16), rope, rope_cols=qw, scale_cols=qw,
                         scale=scale, name=f"proj_q_{l}")
            o = _swa_attention(q, kv, sinks_b[i])
            wo = w_o_b[i]
        x = _attn_out_ffn(x, o, wo.astype(jnp.bfloat16), ffn_norm[l], w_up[l].astype(jnp.bfloat16), conv_w[l],
                          conv_b[l], w_down[l].astype(jnp.bfloat16), final_norm,
                          final=(l == DEPTH - 1), name=f"attn_out_ffn_{l}")
    return x
```

```python
import functools

import jax
import jax.numpy as jnp
from jax import lax
from jax.experimental import pallas as pl
from jax.experimental.pallas import tpu as pltpu

D_MODEL = 1024
DEPTH = 4
N_HEADS = 16
HEAD_DIM = 64
N_KV_HEADS = 4
GROUP = N_HEADS // N_KV_HEADS
ROT_DIM = HEAD_DIM // 4
ROPE_THETA = 500000.0
MOBA_BLOCK = 256
MOBA_TOPK = 3
WINDOW = 128
D_FF = 2816
CONV_W = 3
N_A = DEPTH // 2
EPS = 1e-6
NEG = -1e30
LOWEST_F32 = -3.4028234663852886e38

LANES = 128
ROW_TILE = 1024
PROJ_COL_CHUNK = 512
FF_CHUNK = 256
N_FF_CHUNKS = D_FF // FF_CHUNK
SWA_Q_TILE = 2048
MOBA_PAIRS = 4
MOBA_VROWS = HEAD_DIM + 16
VMEM_LIMIT = 56 * 1024 * 1024

_NT = (((1,), (1,)), ((), ()))


def _rms(x, g):
    ms = jnp.mean(x * x, axis=-1, keepdims=True)
    return x * lax.rsqrt(ms + EPS) * g


def _top16(x):
    bits = lax.bitcast_convert_type(x, jnp.int32) & jnp.int32(-65536)
    return lax.bitcast_convert_type(bits, jnp.float32)


def _const_spec(shape):
    zeros = (0,) * len(shape)
    return pl.BlockSpec(shape, lambda *_: zeros, pipeline_mode=pl.Buffered(1))


def _proj_kernel(x_ref, g_ref, w_ref, rc_ref, ra_ref, rb_ref, *out_refs,
                 n_out, rope_cols, scale_cols, scale, kmean_cols):
    y_ref = out_refs[0]
    h = _rms(x_ref[0], g_ref[...]).astype(jnp.bfloat16)
    rc = rc_ref[0]
    ra = ra_ref[0]
    rb = rb_ref[0]
    ts = h.shape[0]
    for c0 in range(0, n_out, PROJ_COL_CHUNK):
        width = min(PROJ_COL_CHUNK, n_out - c0)
        y = jnp.dot(h, w_ref[:, c0:c0 + width], preferred_element_type=jnp.float32)
        for l0 in range(0, width, LANES):
            col = c0 + l0
            t = y[:, l0:l0 + LANES]
            if col < rope_cols:
                t = (t * rc + pltpu.roll(t, LANES - ROT_DIM // 2, axis=1) * ra
                     + pltpu.roll(t, ROT_DIM // 2, axis=1) * rb)
            if col < scale_cols:
                t = t * scale
            y_ref[0, :, col:col + LANES] = t.astype(y_ref.dtype)
            if kmean_cols is not None and kmean_cols[0] <= col < kmean_cols[1]:
                km_ref = out_refs[1]
                kc = col - kmean_cols[0]
                for blk in range(ts // MOBA_BLOCK):
                    seg = t[blk * MOBA_BLOCK:(blk + 1) * MOBA_BLOCK]
                    km_ref[0, blk:blk + 1, kc:kc + LANES] = jnp.mean(seg, axis=0, keepdims=True)


def _project(x, g, w, rope, *, rope_cols, scale_cols=0, scale=1.0, kmean_cols=None, name):
    b, s, d = x.shape
    n_out = w.shape[1]
    rc, ra, rb = rope
    blocks_per_tile = ROW_TILE // MOBA_BLOCK
    out_shape = [jax.ShapeDtypeStruct((b, s, n_out), jnp.bfloat16)]
    out_specs = [pl.BlockSpec((1, ROW_TILE, n_out), lambda i, j: (i, j, 0))]
    if kmean_cols is not None:
        kw = kmean_cols[1] - kmean_cols[0]
        out_shape.append(jax.ShapeDtypeStruct((b * (s // ROW_TILE), blocks_per_tile, kw), jnp.float32))
        out_specs.append(pl.BlockSpec((1, blocks_per_tile, kw), lambda i, j: (i * (s // ROW_TILE) + j, 0, 0)))
    tab_spec = pl.BlockSpec((1, ROW_TILE, LANES), lambda i, j: (i, j, 0))
    outs = pl.pallas_call(
        functools.partial(_proj_kernel, n_out=n_out, rope_cols=rope_cols, scale_cols=scale_cols,
                          scale=scale, kmean_cols=kmean_cols),
        grid=(b, s // ROW_TILE),
        in_specs=[
            pl.BlockSpec((1, ROW_TILE, d), lambda i, j: (i, j, 0)),
            _const_spec((1, d)),
            _const_spec((d, n_out)),
            tab_spec, tab_spec, tab_spec,
        ],
        out_specs=out_specs,
        out_shape=out_shape,
        compiler_params=pltpu.CompilerParams(
            dimension_semantics=("parallel", "parallel"), vmem_limit_bytes=VMEM_LIMIT),
        name=name,
    )(x, g.reshape(1, d), w, rc, ra, rb)
    if kmean_cols is not None:
        return outs[0], outs[1].reshape(b, s // MOBA_BLOCK, kmean_cols[1] - kmean_cols[0])
    return outs[0]


def _moba_kernel(q_ref, k_ref, v_ref, km_ref, o_ref, vt_ref, bias_ref, s_ref, p_ref, m_ref, a_ref, acc_ref, *,
                 n_blocks, n_pairs):
    i = pl.program_id(2)
    cols = 2 * MOBA_BLOCK

    @pl.when(i == 0)
    def _():
        ones = jnp.ones((MOBA_VROWS - HEAD_DIM, MOBA_BLOCK), vt_ref.dtype)
        for c in range(n_blocks):
            sl = slice(c * MOBA_BLOCK, (c + 1) * MOBA_BLOCK)
            for pr in range(n_pairs):
                vt = v_ref[0, sl, pr * LANES:(pr + 1) * LANES].astype(jnp.float32).T.astype(vt_ref.dtype)
                for hh in range(2):
                    hd = 2 * pr + hh
                    vt_ref[hd * MOBA_VROWS:hd * MOBA_VROWS + HEAD_DIM, sl] = vt[hh * HEAD_DIM:(hh + 1) * HEAD_DIM]
                    vt_ref[hd * MOBA_VROWS + HEAD_DIM:(hd + 1) * MOBA_VROWS, sl] = ones

    lane = lax.broadcasted_iota(jnp.int32, (1, LANES), 1)
    head_lanes = (lane < HEAD_DIM, lane >= HEAD_DIM)
    blk = lax.broadcasted_iota(jnp.int32, (n_blocks, cols), 0)
    past = blk < i
    qcat = []
    gates = []
    for pr in range(n_pairs):
        q2 = q_ref[0, :, pr * LANES:(pr + 1) * LANES]
        qc = jnp.concatenate([jnp.where(head_lanes[hh], q2, jnp.zeros_like(q2)) for hh in range(2)], axis=0)
        qc = qc.astype(jnp.float32).T.astype(jnp.bfloat16)
        qcat.append(qc)
        km = km_ref[0, :, pr * LANES:(pr + 1) * LANES]
        hi = _top16(km)
        mid = _top16(km - hi)
        lo = km - hi - mid
        limbs = jnp.concatenate([hi, mid, lo], axis=0).astype(jnp.bfloat16)
        parts = jnp.dot(limbs, qc, preferred_element_type=jnp.float32)
        gates.append(parts[:n_blocks] + parts[n_blocks:2 * n_blocks] + parts[2 * n_blocks:])

    def scores(j, pr):
        start = pl.multiple_of(j * MOBA_BLOCK, MOBA_BLOCK)
        kj = k_ref[0, pl.ds(start, MOBA_BLOCK), pr * LANES:(pr + 1) * LANES]
        return jnp.dot(kj, qcat[pr], preferred_element_type=jnp.float32)

    def weighted_values(j, pr, pb):
        start = pl.multiple_of(j * MOBA_BLOCK, MOBA_BLOCK)
        outs = []
        for hh in range(2):
            r0 = (2 * pr + hh) * MOBA_VROWS
            vt = vt_ref[r0:r0 + MOBA_VROWS, pl.ds(start, MOBA_BLOCK)]
            outs.append(jnp.dot(vt, pb[:, hh * MOBA_BLOCK:(hh + 1) * MOBA_BLOCK],
                                preferred_element_type=jnp.float32))
        return jnp.concatenate(outs, axis=1)

    def rescaled(acc, a):
        return jnp.maximum(acc * a, LOWEST_F32)

    key_i = lax.broadcasted_iota(jnp.int32, (MOBA_BLOCK, cols), 0)
    qry_i = lax.broadcasted_iota(jnp.int32, (MOBA_BLOCK, cols), 1) & (MOBA_BLOCK - 1)
    causal = key_i <= qry_i
    own = [scores(i, pr) for pr in range(n_pairs)]
    for pr in range(n_pairs):
        s_ref[pr] = scores(0, pr)
    for pr in range(n_pairs):
        g = jnp.where(past, gates[pr], NEG)
        rank = jnp.zeros((n_blocks, cols), jnp.int32)
        for jp in range(n_blocks):
            gj = g[jp:jp + 1, :]
            ahead = (gj > g) | ((gj == g) & (jp < blk))
            rank = rank + ahead.astype(jnp.int32)
        sel = past & (rank < MOBA_TOPK)
        bias_ref[pr] = jnp.where(sel, 0.0, NEG)
    for pr in range(n_pairs):
        s = jnp.where(causal, own[pr], NEG)
        m = jnp.max(s, axis=0, keepdims=True)
        p_ref[pr] = jnp.exp(s - m).astype(p_ref.dtype)
        m_ref[pr] = m
        a_ref[pr] = jnp.ones((1, cols), jnp.float32)
        acc_ref[pr] = jnp.zeros((MOBA_VROWS, cols), jnp.float32)

    def body(t, carry):
        cur = [s_ref[pr] for pr in range(n_pairs)]
        p_prev = [p_ref[pr] for pr in range(n_pairs)]
        state = [(m_ref[pr], a_ref[pr], acc_ref[pr]) for pr in range(n_pairs)]
        row = [bias_ref[pr, pl.ds(t - 1, 1), :] for pr in range(n_pairs)]
        prev_blk = jnp.where(t == 1, i, t - 2)
        nxt = [scores(t, pr) for pr in range(n_pairs)]
        pv = [weighted_values(prev_blk, pr, p_prev[pr]) for pr in range(n_pairs)]
        for pr in range(n_pairs):
            m_old, a_prev, acc_old = state[pr]
            m_new = jnp.maximum(m_old, jnp.max(cur[pr], axis=0, keepdims=True) + row[pr])
            p_ref[pr] = jnp.exp(cur[pr] - (m_new - row[pr])).astype(p_ref.dtype)
            a_ref[pr] = jnp.exp(m_old - m_new)
            m_ref[pr] = m_new
            acc_ref[pr] = rescaled(acc_old, a_prev) + pv[pr]
            s_ref[pr] = nxt[pr]
        return carry

    lax.fori_loop(1, i + 1, body, 0)

    last_blk = jnp.maximum(i - 1, 0)
    for pr in range(n_pairs):
        acc = rescaled(acc_ref[pr], a_ref[pr]) + weighted_values(last_blk, pr, p_ref[pr])
        out = acc[:HEAD_DIM] / acc[HEAD_DIM:HEAD_DIM + 1]
        out_t = jnp.concatenate([out[:, :MOBA_BLOCK], out[:, MOBA_BLOCK:]], axis=0)
        o_ref[0, :, pr * LANES:(pr + 1) * LANES] = out_t.T.astype(o_ref.dtype)


def _moba_attention(qkv, kmean):
    b, s, _ = qkv.shape
    n_blocks = s // MOBA_BLOCK
    width = MOBA_PAIRS * LANES
    groups = N_HEADS * HEAD_DIM // width
    cols = 2 * MOBA_BLOCK
    return pl.pallas_call(
        functools.partial(_moba_kernel, n_blocks=n_blocks, n_pairs=MOBA_PAIRS),
        grid=(b, groups, n_blocks),
        in_specs=[
            pl.BlockSpec((1, MOBA_BLOCK, width), lambda bi, g, i: (bi, i, g)),
            pl.BlockSpec((1, s, width), lambda bi, g, i: (bi, 0, groups + g)),
            pl.BlockSpec((1, s, width), lambda bi, g, i: (bi, 0, 2 * groups + g)),
            pl.BlockSpec((1, n_blocks, width), lambda bi, g, i: (bi, 0, g)),
        ],
        out_specs=pl.BlockSpec((1, MOBA_BLOCK, width), lambda bi, g, i: (bi, i, g)),
        out_shape=jax.ShapeDtypeStruct((b, s, N_HEADS * HEAD_DIM), jnp.bfloat16),
        scratch_shapes=[
            pltpu.VMEM((2 * MOBA_PAIRS * MOBA_VROWS, s), jnp.bfloat16),
            pltpu.VMEM((MOBA_PAIRS, n_blocks, cols), jnp.float32),
            pltpu.VMEM((MOBA_PAIRS, MOBA_BLOCK, cols), jnp.float32),
            pltpu.VMEM((MOBA_PAIRS, MOBA_BLOCK, cols), jnp.bfloat16),
            pltpu.VMEM((MOBA_PAIRS, 1, cols), jnp.float32),
            pltpu.VMEM((MOBA_PAIRS, 1, cols), jnp.float32),
            pltpu.VMEM((MOBA_PAIRS, MOBA_VROWS, cols), jnp.float32),
        ],
        compiler_params=pltpu.CompilerParams(
            dimension_semantics=("parallel", "parallel", "arbitrary"), vmem_limit_bytes=VMEM_LIMIT),
        name="moba_attention",
    )(qkv, qkv, qkv, kmean)


def _swa_kernel(sink_ref, q_ref, k_ref, v_ref, o_ref):
    a = pl.program_id(1)
    t = pl.program_id(2)
    lane = lax.broadcasted_iota(jnp.int32, (1, LANES), 1)
    half_lanes = (lane < HEAD_DIM, lane >= HEAD_DIM)
    r = lax.broadcasted_iota(jnp.int32, (WINDOW, 2 * WINDOW), 0)
    c = lax.broadcasted_iota(jnp.int32, (WINDOW, 2 * WINDOW), 1)
    band = (c > r) & (c <= r + WINDOW)
    own_only = c >= WINDOW
    for w in range(SWA_Q_TILE // WINDOW):
        n = t * (SWA_Q_TILE // WINDOW) + w
        own0 = pl.multiple_of(n * WINDOW, WINDOW)
        prev0 = pl.multiple_of(jnp.maximum(n - 1, 0) * WINDOW, WINDOW)
        kd = jnp.concatenate([k_ref[0, pl.ds(prev0, WINDOW), :], k_ref[0, pl.ds(own0, WINDOW), :]], axis=0)
        vd = jnp.concatenate([v_ref[0, pl.ds(prev0, WINDOW), :], v_ref[0, pl.ds(own0, WINDOW), :]], axis=0)
        k_half = [jnp.where(half_lanes[e], kd, jnp.zeros_like(kd)) for e in range(2)]
        v_half = [jnp.where(half_lanes[e], vd, jnp.zeros_like(vd)) for e in range(2)]
        mask = band & (own_only | (n > 0))
        for u in range(GROUP * HEAD_DIM // LANES):
            q2 = q_ref[0, w * WINDOW:(w + 1) * WINDOW, u * LANES:(u + 1) * LANES]
            out = None
            for e in range(2):
                sink = sink_ref[a * GROUP + 2 * u + e]
                s = lax.dot_general(q2, k_half[e], _NT, preferred_element_type=jnp.float32)
                s = jnp.where(mask, s, NEG)
                m = jnp.maximum(jnp.max(s, axis=1, keepdims=True), sink)
                p = jnp.exp(s - m)
                denom = jnp.sum(p, axis=1, keepdims=True) + jnp.exp(sink - m)
                pv = jnp.dot(p.astype(jnp.bfloat16), v_half[e], preferred_element_type=jnp.float32) / denom
                out = pv if out is None else out + pv
            o_ref[0, w * WINDOW:(w + 1) * WINDOW, u * LANES:(u + 1) * LANES] = out.astype(o_ref.dtype)


def _swa_attention(q, kv, sinks):
    b, s, _ = q.shape
    gw = GROUP * HEAD_DIM
    grid_spec = pltpu.PrefetchScalarGridSpec(
        num_scalar_prefetch=1,
        grid=(b, N_KV_HEADS, s // SWA_Q_TILE),
        in_specs=[
            pl.BlockSpec((1, SWA_Q_TILE, gw), lambda bi, a, t, sk: (bi, t, a)),
            pl.BlockSpec((1, s, LANES), lambda bi, a, t, sk: (bi, 0, a)),
            pl.BlockSpec((1, s, LANES), lambda bi, a, t, sk: (bi, 0, N_KV_HEADS + a)),
        ],
        out_specs=pl.BlockSpec((1, SWA_Q_TILE, gw), lambda bi, a, t, sk: (bi, t, a)),
    )
    return pl.pallas_call(
        _swa_kernel,
        grid_spec=grid_spec,
        out_shape=jax.ShapeDtypeStruct((b, s, N_HEADS * HEAD_DIM), jnp.bfloat16),
        compiler_params=pltpu.CompilerParams(
            dimension_semantics=("parallel", "parallel", "parallel"), vmem_limit_bytes=VMEM_LIMIT),
        name="swa_attention",
    )(sinks, q, kv, kv)


def _ffn_kernel(x_ref, o_ref, wo_ref, g_ref, wu_ref, cw_ref, cb_ref, wd_ref, gf_ref, out_ref,
                h_ref, act_ref, carry_ref, *, final):
    ts = x_ref.shape[1]

    @pl.when(pl.program_id(1) == 0)
    def _():
        carry_ref[...] = jnp.zeros_like(carry_ref)

    x1 = x_ref[0] + jnp.dot(o_ref[0], wo_ref[...], preferred_element_type=jnp.float32)
    out_ref[0] = x1
    h_ref[...] = _rms(x1, g_ref[...]).astype(jnp.bfloat16)
    row = lax.broadcasted_iota(jnp.int32, (ts, 1), 0)

    def conv(u, tail, taps, bias):
        last1 = tail[7:8]
        last2 = tail[6:7]
        u1 = jnp.where(row == 0, last1, pltpu.roll(u, 1, axis=0))
        u2 = jnp.where(row == 0, last2, jnp.where(row == 1, last1, pltpu.roll(u, 2, axis=0)))
        return bias + taps[0:1] * u2 + taps[1:2] * u1 + taps[2:3] * u

    def cols(c, half):
        start = half * D_FF + c * FF_CHUNK
        return slice(start, start + FF_CHUNK)

    def up(c):
        h = h_ref[...]
        return tuple(jnp.dot(h, wu_ref[:, cols(c, half)], preferred_element_type=jnp.float32)
                     for half in range(2))

    def mix(c, ug, uv):
        gate = conv(ug, carry_ref[c, 0], cw_ref[:, cols(c, 0)], cb_ref[:, cols(c, 0)])
        val = conv(uv, carry_ref[c, 1], cw_ref[:, cols(c, 1)], cb_ref[:, cols(c, 1)])
        carry_ref[c, 0] = ug[ts - 8:]
        carry_ref[c, 1] = uv[ts - 8:]
        return (gate / (1.0 + jnp.exp(-gate)) * val).astype(jnp.bfloat16)

    nxt = up(0)
    for c in range(N_FF_CHUNKS):
        cur = nxt
        if c + 1 < N_FF_CHUNKS:
            nxt = up(c + 1)
        act_ref[:, c * FF_CHUNK:(c + 1) * FF_CHUNK] = mix(c, *cur)
    y = out_ref[0] + jnp.dot(act_ref[...], wd_ref[...], preferred_element_type=jnp.float32)
    out_ref[0] = _rms(y, gf_ref[...]) if final else y


def _attn_out_ffn(x, o, wo, g, wu, cw, cb, wd, gf, *, final, name):
    b, s, d = x.shape
    row_spec = pl.BlockSpec((1, ROW_TILE, d), lambda i, j: (i, j, 0))
    return pl.pallas_call(
        functools.partial(_ffn_kernel, final=final),
        grid=(b, s // ROW_TILE),
        in_specs=[
            row_spec, row_spec,
            _const_spec((d, d)),
            _const_spec((1, d)),
            _const_spec((d, 2 * D_FF)),
            _const_spec((CONV_W, 2 * D_FF)),
            _const_spec((1, 2 * D_FF)),
            _const_spec((D_FF, d)),
            _const_spec((1, d)),
        ],
        out_specs=row_spec,
        out_shape=jax.ShapeDtypeStruct((b, s, d), jnp.float32),
        scratch_shapes=[
            pltpu.VMEM((ROW_TILE, d), jnp.bfloat16),
            pltpu.VMEM((ROW_TILE, D_FF), jnp.bfloat16),
            pltpu.VMEM((N_FF_CHUNKS, 2, 8, FF_CHUNK), jnp.float32),
        ],
        compiler_params=pltpu.CompilerParams(
            dimension_semantics=("parallel", "arbitrary"), vmem_limit_bytes=VMEM_LIMIT),
        name=name,
    )(x, o, wo, g.reshape(1, d), wu, cw, cb.reshape(1, 2 * D_FF), wd, gf.reshape(1, d))


def _rope_tables(positions):
    inv_freq = ROPE_THETA ** (-jnp.arange(0, ROT_DIM, 2, dtype=jnp.float32) / ROT_DIM)
    ang = positions.astype(jnp.float32)[..., None] * inv_freq
    cos, sin = jnp.cos(ang), jnp.sin(ang)
    half = ROT_DIM // 2
    rest = HEAD_DIM - ROT_DIM
    shape = ang.shape[:-1]
    ones = jnp.ones(shape + (rest,), jnp.float32)
    z_half = jnp.zeros(shape + (half,), jnp.float32)
    z_rest = jnp.zeros(shape + (rest,), jnp.float32)
    rc = jnp.concatenate([cos, cos, ones], axis=-1)
    ra = jnp.concatenate([-sin, z_half, z_rest], axis=-1)
    rb = jnp.concatenate([z_half, sin, z_rest], axis=-1)
    reps = LANES // HEAD_DIM
    return tuple(jnp.tile(t, (1, 1, reps)) for t in (rc, ra, rb))


def _dup_heads(w):
    d = w.shape[0]
    w = w.reshape(d, N_KV_HEADS, 1, HEAD_DIM)
    return jnp.broadcast_to(w, (d, N_KV_HEADS, LANES // HEAD_DIM, HEAD_DIM)).reshape(d, N_KV_HEADS * LANES)


def kernel(x, positions, attn_norm, w_qkv_a, w_o_a, kv_norm, w_kv_b, w_q_b, sinks_b, w_o_b, ffn_norm,
           w_up, conv_w, conv_b, w_down, final_norm):
    assert x.shape[1] % ROW_TILE == 0 and x.shape[1] % SWA_Q_TILE == 0 and x.shape[2] == D_MODEL
    rope = _rope_tables(positions)
    scale = HEAD_DIM ** -0.5
    qw = N_HEADS * HEAD_DIM
    kvw = N_KV_HEADS * HEAD_DIM
    kv = None
    for l in range(DEPTH):
        if l < N_A:
            qkv, kmean = _project(
                x, attn_norm[l], w_qkv_a[l].astype(jnp.bfloat16), rope, rope_cols=2 * qw, scale_cols=qw,
                scale=scale, kmean_cols=(qw, 2 * qw), name=f"proj_qkv_{l}")
            o = _moba_attention(qkv, kmean)
            wo = w_o_a[l]
        else:
            i = l - N_A
            if l == N_A:
                w_kv = jnp.concatenate([_dup_heads(w_kv_b[:, :kvw]), _dup_heads(w_kv_b[:, kvw:])], axis=1)
                kv = _project(x, kv_norm, w_kv.astype(jnp.bfloat16), rope, rope_cols=N_KV_HEADS * LANES,
                              name="proj_kv")
            q = _project(x, attn_norm[l], w_q_b[i].astype(jnp.bfloat16), rope, rope_cols=qw, scale_cols=qw,
                         scale=scale, name=f"proj_q_{l}")
            o = _swa_attention(q, kv, sinks_b[i])
            wo = w_o_b[i]
        x = _attn_out_ffn(x, o, wo.astype(jnp.bfloat16), ffn_norm[l], w_up[l].astype(jnp.bfloat16), conv_w[l],
                          conv_b[l], w_down[l].astype(jnp.bfloat16), final_norm,
                          final=(l == DEPTH - 1), name=f"attn_out_ffn_{l}")
    return x
```

```python
import functools

import jax
import jax.numpy as jnp
from jax import lax
from jax.experimental import pallas as pl
from jax.experimental.pallas import tpu as pltpu

D_MODEL = 1024
DEPTH = 4
N_HEADS = 16
HEAD_DIM = 64
N_KV_HEADS = 4
GROUP = N_HEADS // N_KV_HEADS
ROT_DIM = HEAD_DIM // 4
ROPE_THETA = 500000.0
MOBA_BLOCK = 256
MOBA_TOPK = 3
WINDOW = 128
D_FF = 2816
CONV_W = 3
N_A = DEPTH // 2
EPS = 1e-6
NEG = -1e30
LOWEST_F32 = -3.4028234663852886e38

LANES = 128
ROW_TILE = 1024
PROJ_COL_CHUNK = 512
FF_CHUNK = 256
N_FF_CHUNKS = D_FF // FF_CHUNK
SWA_Q_TILE = 2048
MOBA_PAIRS = 4
MOBA_VROWS = HEAD_DIM + 16
VMEM_LIMIT = 56 * 1024 * 1024

_NT = (((1,), (1,)), ((), ()))


def _rms(x, g):
    ms = jnp.mean(x * x, axis=-1, keepdims=True)
    return x * lax.rsqrt(ms + EPS) * g


def _top16(x):
    bits = lax.bitcast_convert_type(x, jnp.int32) & jnp.int32(-65536)
    return lax.bitcast_convert_type(bits, jnp.float32)


def _const_spec(shape):
    zeros = (0,) * len(shape)
    return pl.BlockSpec(shape, lambda *_: zeros, pipeline_mode=pl.Buffered(1))


def _proj_kernel(x_ref, g_ref, w_ref, rc_ref, ra_ref, rb_ref, *out_refs,
                 n_out, rope_cols, scale_cols, scale, kmean_cols):
    y_ref = out_refs[0]
    h = _rms(x_ref[0], g_ref[...]).astype(jnp.bfloat16)
    rc = rc_ref[0]
    ra = ra_ref[0]
    rb = rb_ref[0]
    ts = h.shape[0]
    for c0 in range(0, n_out, PROJ_COL_CHUNK):
        width = min(PROJ_COL_CHUNK, n_out - c0)
        y = jnp.dot(h, w_ref[:, c0:c0 + width], preferred_element_type=jnp.float32)
        for l0 in range(0, width, LANES):
            col = c0 + l0
            t = y[:, l0:l0 + LANES]
            if col < rope_cols:
                t = (t * rc + pltpu.roll(t, LANES - ROT_DIM // 2, axis=1) * ra
                     + pltpu.roll(t, ROT_DIM // 2, axis=1) * rb)
            if col < scale_cols:
                t = t * scale
            y_ref[0, :, col:col + LANES] = t.astype(y_ref.dtype)
            if kmean_cols is not None and kmean_cols[0] <= col < kmean_cols[1]:
                km_ref = out_refs[1]
                kc = col - kmean_cols[0]
                for blk in range(ts // MOBA_BLOCK):
                    seg = t[blk * MOBA_BLOCK:(blk + 1) * MOBA_BLOCK]
                    km_ref[0, blk:blk + 1, kc:kc + LANES] = jnp.mean(seg, axis=0, keepdims=True)


def _project(x, g, w, rope, *, rope_cols, scale_cols=0, scale=1.0, kmean_cols=None, name):
    b, s, d = x.shape
    n_out = w.shape[1]
    rc, ra, rb = rope
    blocks_per_tile = ROW_TILE // MOBA_BLOCK
    out_shape = [jax.ShapeDtypeStruct((b, s, n_out), jnp.bfloat16)]
    out_specs = [pl.BlockSpec((1, ROW_TILE, n_out), lambda i, j: (i, j, 0))]
    if kmean_cols is not None:
        kw = kmean_cols[1] - kmean_cols[0]
        out_shape.append(jax.ShapeDtypeStruct((b * (s // ROW_TILE), blocks_per_tile, kw), jnp.float32))
        out_specs.append(pl.BlockSpec((1, blocks_per_tile, kw), lambda i, j: (i * (s // ROW_TILE) + j, 0, 0)))
    tab_spec = pl.BlockSpec((1, ROW_TILE, LANES), lambda i, j: (i, j, 0))
    outs = pl.pallas_call(
        functools.partial(_proj_kernel, n_out=n_out, rope_cols=rope_cols, scale_cols=scale_cols,
                          scale=scale, kmean_cols=kmean_cols),
        grid=(b, s // ROW_TILE),
        in_specs=[
            pl.BlockSpec((1, ROW_TILE, d), lambda i, j: (i, j, 0)),
            _const_spec((1, d)),
            _const_spec((d, n_out)),
            tab_spec, tab_spec, tab_spec,
        ],
        out_specs=out_specs,
        out_shape=out_shape,
        compiler_params=pltpu.CompilerParams(
            dimension_semantics=("parallel", "parallel"), vmem_limit_bytes=VMEM_LIMIT),
        name=name,
    )(x, g.reshape(1, d), w, rc, ra, rb)
    if kmean_cols is not None:
        return outs[0], outs[1].reshape(b, s // MOBA_BLOCK, kmean_cols[1] - kmean_cols[0])
    return outs[0]


def _moba_kernel(q_ref, k_ref, v_ref, km_ref, o_ref, vt_ref, bias_ref, s_ref, p_ref, m_ref, a_ref, acc_ref, *,
                 n_blocks, n_pairs):
    i = pl.program_id(2)
    cols = 2 * MOBA_BLOCK

    @pl.when(i == 0)
    def _():
        ones = jnp.ones((MOBA_VROWS - HEAD_DIM, MOBA_BLOCK), vt_ref.dtype)
        for c in range(n_blocks):
            sl = slice(c * MOBA_BLOCK, (c + 1) * MOBA_BLOCK)
            for pr in range(n_pairs):
                vt = v_ref[0, sl, pr * LANES:(pr + 1) * LANES].astype(jnp.float32).T.astype(vt_ref.dtype)
                for hh in range(2):
                    hd = 2 * pr + hh
                    vt_ref[hd * MOBA_VROWS:hd * MOBA_VROWS + HEAD_DIM, sl] = vt[hh * HEAD_DIM:(hh + 1) * HEAD_DIM]
                    vt_ref[hd * MOBA_VROWS + HEAD_DIM:(hd + 1) * MOBA_VROWS, sl] = ones

    lane = lax.broadcasted_iota(jnp.int32, (1, LANES), 1)
    head_lanes = (lane < HEAD_DIM, lane >= HEAD_DIM)
    blk = lax.broadcasted_iota(jnp.int32, (n_blocks, cols), 0)
    past = blk < i
    qcat = []
    gates = []
    for pr in range(n_pairs):
        q2 = q_ref[0, :, pr * LANES:(pr + 1) * LANES]
        qc = jnp.concatenate([jnp.where(head_lanes[hh], q2, jnp.zeros_like(q2)) for hh in range(2)], axis=0)
        qc = qc.astype(jnp.float32).T.astype(jnp.bfloat16)
        qcat.append(qc)
        km = km_ref[0, :, pr * LANES:(pr + 1) * LANES]
        hi = _top16(km)
        mid = _top16(km - hi)
        lo = km - hi - mid
        limbs = jnp.concatenate([hi, mid, lo], axis=0).astype(jnp.bfloat16)
        parts = jnp.dot(limbs, qc, preferred_element_type=jnp.float32)
        gates.append(parts[:n_blocks] + parts[n_blocks:2 * n_blocks] + parts[2 * n_blocks:])

    def scores(j, pr):
        start = pl.multiple_of(j * MOBA_BLOCK, MOBA_BLOCK)
        kj = k_ref[0, pl.ds(start, MOBA_BLOCK), pr * LANES:(pr + 1) * LANES]
        return jnp.dot(kj, qcat[pr], preferred_element_type=jnp.float32)

    def weighted_values(j, pr, pb):
        start = pl.multiple_of(j * MOBA_BLOCK, MOBA_BLOCK)
        outs = []
        for hh in range(2):
            r0 = (2 * pr + hh) * MOBA_VROWS
            vt = vt_ref[r0:r0 + MOBA_VROWS, pl.ds(start, MOBA_BLOCK)]
            outs.append(jnp.dot(vt, pb[:, hh * MOBA_BLOCK:(hh + 1) * MOBA_BLOCK],
                                preferred_element_type=jnp.float32))
        return jnp.concatenate(outs, axis=1)

    def rescaled(acc, a):
        return jnp.maximum(acc * a, LOWEST_F32)

    key_i = lax.broadcasted_iota(jnp.int32, (MOBA_BLOCK, cols), 0)
    qry_i = lax.broadcasted_iota(jnp.int32, (MOBA_BLOCK, cols), 1) & (MOBA_BLOCK - 1)
    causal = key_i <= qry_i
    own = [scores(i, pr) for pr in range(n_pairs)]
    for pr in range(n_pairs):
        s_ref[pr] = scores(0, pr)
    for pr in range(n_pairs):
        g = jnp.where(past, gates[pr], NEG)
        rank = jnp.zeros((n_blocks, cols), jnp.int32)
        for jp in range(n_blocks):
            gj = g[jp:jp + 1, :]
            ahead = (gj > g) | ((gj == g) & (jp < blk))
            rank = rank + ahead.astype(jnp.int32)
        sel = past & (rank < MOBA_TOPK)
        bias_ref[pr] = jnp.where(sel, 0.0, NEG)
    for pr in range(n_pairs):
        s = jnp.where(causal, own[pr], NEG)
        m = jnp.max(s, axis=0, keepdims=True)
        p_ref[pr] = jnp.exp(s - m).astype(p_ref.dtype)
        m_ref[pr] = m
        a_ref[pr] = jnp.ones((1, cols), jnp.float32)
        acc_ref[pr] = jnp.zeros((MOBA_VROWS, cols), jnp.float32)

    def body(t, carry):
        cur = [s_ref[pr] for pr in range(n_pairs)]
        p_prev = [p_ref[pr] for pr in range(n_pairs)]
        state = [(m_ref[pr], a_ref[pr], acc_ref[pr]) for pr in range(n_pairs)]
        row = [bias_ref[pr, pl.ds(t - 1, 1), :] for pr in range(n_pairs)]
        prev_blk = jnp.where(t == 1, i, t - 2)
        nxt = [scores(t, pr) for pr in range(n_pairs)]
        pv = [weighted_values(prev_blk, pr, p_prev[pr]) for pr in range(n_pairs)]
        for pr in range(n_pairs):
            m_old, a_prev, acc_old = state[pr]
            m_new = jnp.maximum(m_old, jnp.max(cur[pr], axis=0, keepdims=True) + row[pr])
            p_ref[pr] = jnp.exp(cur[pr] - (m_new - row[pr])).astype(p_ref.dtype)
            a_ref[pr] = jnp.exp(m_old - m_new)
            m_ref[pr] = m_new
            acc_ref[pr] = rescaled(acc_old, a_prev) + pv[pr]
            s_ref[pr] = nxt[pr]
        return carry

    lax.fori_loop(1, i + 1, body, 0)

    last_blk = jnp.maximum(i - 1, 0)
    for pr in range(n_pairs):
        acc = rescaled(acc_ref[pr], a_ref[pr]) + weighted_values(last_blk, pr, p_ref[pr])
        out = acc[:HEAD_DIM] / acc[HEAD_DIM:HEAD_DIM + 1]
        out_t = jnp.concatenate([out[:, :MOBA_BLOCK], out[:, MOBA_BLOCK:]], axis=0)
        o_ref[0, :, pr * LANES:(pr + 1) * LANES] = out_t.T.astype(o_ref.dtype)


def _moba_attention(qkv, kmean):
    b, s, _ = qkv.shape
    n_blocks = s // MOBA_BLOCK
    width = MOBA_PAIRS * LANES
    groups = N_HEADS * HEAD_DIM // width
    cols = 2 * MOBA_BLOCK
    return pl.pallas_call(
        functools.partial(_moba_kernel, n_blocks=n_blocks, n_pairs=MOBA_PAIRS),
        grid=(b, groups, n_blocks),
        in_specs=[
            pl.BlockSpec((1, MOBA_BLOCK, width), lambda bi, g, i: (bi, i, g)),
            pl.BlockSpec((1, s, width), lambda bi, g, i: (bi, 0, groups + g)),
            pl.BlockSpec((1, s, width), lambda bi, g, i: (bi, 0, 2 * groups + g)),
            pl.BlockSpec((1, n_blocks, width), lambda bi, g, i: (bi, 0, g)),
        ],
        out_specs=pl.BlockSpec((1, MOBA_BLOCK, width), lambda bi, g, i: (bi, i, g)),
        out_shape=jax.ShapeDtypeStruct((b, s, N_HEADS * HEAD_DIM), jnp.bfloat16),
        scratch_shapes=[
            pltpu.VMEM((2 * MOBA_PAIRS * MOBA_VROWS, s), jnp.bfloat16),
            pltpu.VMEM((MOBA_PAIRS, n_blocks, cols), jnp.float32),
            pltpu.VMEM((MOBA_PAIRS, MOBA_BLOCK, cols), jnp.float32),
            pltpu.VMEM((MOBA_PAIRS, MOBA_BLOCK, cols), jnp.bfloat16),
            pltpu.VMEM((MOBA_PAIRS, 1, cols), jnp.float32),
            pltpu.VMEM((MOBA_PAIRS, 1, cols), jnp.float32),
            pltpu.VMEM((MOBA_PAIRS, MOBA_VROWS, cols), jnp.float32),
        ],
        compiler_params=pltpu.CompilerParams(
            dimension_semantics=("parallel", "parallel", "arbitrary"), vmem_limit_bytes=VMEM_LIMIT),
        name="moba_attention",
    )(qkv, qkv, qkv, kmean)


def _swa_kernel(sink_ref, q_ref, k_ref, v_ref, o_ref):
    a = pl.program_id(1)
    t = pl.program_id(2)
    lane = lax.broadcasted_iota(jnp.int32, (1, LANES), 1)
    half_lanes = (lane < HEAD_DIM, lane >= HEAD_DIM)
    r = lax.broadcasted_iota(jnp.int32, (WINDOW, 2 * WINDOW), 0)
    c = lax.broadcasted_iota(jnp.int32, (WINDOW, 2 * WINDOW), 1)
    band = (c > r) & (c <= r + WINDOW)
    own_only = c >= WINDOW
    for w in range(SWA_Q_TILE // WINDOW):
        n = t * (SWA_Q_TILE // WINDOW) + w
        own0 = pl.multiple_of(n * WINDOW, WINDOW)
        prev0 = pl.multiple_of(jnp.maximum(n - 1, 0) * WINDOW, WINDOW)
        kd = jnp.concatenate([k_ref[0, pl.ds(prev0, WINDOW), :], k_ref[0, pl.ds(own0, WINDOW), :]], axis=0)
        vd = jnp.concatenate([v_ref[0, pl.ds(prev0, WINDOW), :], v_ref[0, pl.ds(own0, WINDOW), :]], axis=0)
        k_half = [jnp.where(half_lanes[e], kd, jnp.zeros_like(kd)) for e in range(2)]
        v_half = [jnp.where(half_lanes[e], vd, jnp.zeros_like(vd)) for e in range(2)]
        mask = band & (own_only | (n > 0))
        for u in range(GROUP * HEAD_DIM // LANES):
            q2 = q_ref[0, w * WINDOW:(w + 1) * WINDOW, u * LANES:(u + 1) * LANES]
            out = None
            for e in range(2):
                sink = sink_ref[a * GROUP + 2 * u + e]
                s = lax.dot_general(q2, k_half[e], _NT, preferred_element_type=jnp.float32)
                s = jnp.where(mask, s, NEG)
                m = jnp.maximum(jnp.max(s, axis=1, keepdims=True), sink)
                p = jnp.exp(s - m)
                denom = jnp.sum(p, axis=1, keepdims=True) + jnp.exp(sink - m)
                pv = jnp.dot(p.astype(jnp.bfloat16), v_half[e], preferred_element_type=jnp.float32) / denom
                out = pv if out is None else out + pv
            o_ref[0, w * WINDOW:(w + 1) * WINDOW, u * LANES:(u + 1) * LANES] = out.astype(o_ref.dtype)


def _swa_attention(q, kv, sinks):
    b, s, _ = q.shape
    gw = GROUP * HEAD_DIM
    grid_spec = pltpu.PrefetchScalarGridSpec(
        num_scalar_prefetch=1,
        grid=(b, N_KV_HEADS, s // SWA_Q_TILE),
        in_specs=[
            pl.BlockSpec((1, SWA_Q_TILE, gw), lambda bi, a, t, sk: (bi, t, a)),
            pl.BlockSpec((1, s, LANES), lambda bi, a, t, sk: (bi, 0, a)),
            pl.BlockSpec((1, s, LANES), lambda bi, a, t, sk: (bi, 0, N_KV_HEADS + a)),
        ],
        out_specs=pl.BlockSpec((1, SWA_Q_TILE, gw), lambda bi, a, t, sk: (bi, t, a)),
    )
    return pl.pallas_call(
        _swa_kernel,
        grid_spec=grid_spec,
        out_shape=jax.ShapeDtypeStruct((b, s, N_HEADS * HEAD_DIM), jnp.bfloat16),
        compiler_params=pltpu.CompilerParams(
            dimension_semantics=("parallel", "parallel", "parallel"), vmem_limit_bytes=VMEM_LIMIT),
        name="swa_attention",
    )(sinks, q, kv, kv)


def _ffn_kernel(x_ref, o_ref, wo_ref, g_ref, wu_ref, cw_ref, cb_ref, wd_ref, gf_ref, out_ref,
                h_ref, act_ref, carry_ref, *, final):
    ts = x_ref.shape[1]

    @pl.when(pl.program_id(1) == 0)
    def _():
        carry_ref[...] = jnp.zeros_like(carry_ref)

    x1 = x_ref[0] + jnp.dot(o_ref[0], wo_ref[...], preferred_element_type=jnp.float32)
    out_ref[0] = x1
    h_ref[...] = _rms(x1, g_ref[...]).astype(jnp.bfloat16)
    row = lax.broadcasted_iota(jnp.int32, (ts, 1), 0)

    def conv(u, tail, taps, bias):
        last1 = tail[7:8]
        last2 = tail[6:7]
        u1 = jnp.where(row == 0, last1, pltpu.roll(u, 1, axis=0))
        u2 = jnp.where(row == 0, last2, jnp.where(row == 1, last1, pltpu.roll(u, 2, axis=0)))
        return bias + taps[0:1] * u2 + taps[1:2] * u1 + taps[2:3] * u

    def cols(c, half):
        start = half * D_FF + c * FF_CHUNK
        return slice(start, start + FF_CHUNK)

    def up(c):
        h = h_ref[...]
        return tuple(jnp.dot(h, wu_ref[:, cols(c, half)], preferred_element_type=jnp.float32)
                     for half in range(2))

    def mix(c, ug, uv):
        gate = conv(ug, carry_ref[c, 0], cw_ref[:, cols(c, 0)], cb_ref[:, cols(c, 0)])
        val = conv(uv, carry_ref[c, 1], cw_ref[:, cols(c, 1)], cb_ref[:, cols(c, 1)])
        carry_ref[c, 0] = ug[ts - 8:]
        carry_ref[c, 1] = uv[ts - 8:]
        half = 0.5 * gate
        return (half * (jnp.tanh(half) + 1.0) * val).astype(jnp.bfloat16)

    nxt = up(0)
    for c in range(N_FF_CHUNKS):
        cur = nxt
        if c + 1 < N_FF_CHUNKS:
            nxt = up(c + 1)
        act_ref[:, c * FF_CHUNK:(c + 1) * FF_CHUNK] = mix(c, *cur)
    y = out_ref[0] + jnp.dot(act_ref[...], wd_ref[...], preferred_element_type=jnp.float32)
    out_ref[0] = _rms(y, gf_ref[...]) if final else y


def _attn_out_ffn(x, o, wo, g, wu, cw, cb, wd, gf, *, final, name):
    b, s, d = x.shape
    row_spec = pl.BlockSpec((1, ROW_TILE, d), lambda i, j: (i, j, 0))
    return pl.pallas_call(
        functools.partial(_ffn_kernel, final=final),
        grid=(b, s // ROW_TILE),
        in_specs=[
            row_spec, row_spec,
            _const_spec((d, d)),
            _const_spec((1, d)),
            _const_spec((d, 2 * D_FF)),
            _const_spec((CONV_W, 2 * D_FF)),
            _const_spec((1, 2 * D_FF)),
            _const_spec((D_FF, d)),
            _const_spec((1, d)),
        ],
        out_specs=row_spec,
        out_shape=jax.ShapeDtypeStruct((b, s, d), jnp.float32),
        scratch_shapes=[
            pltpu.VMEM((ROW_TILE, d), jnp.bfloat16),
            pltpu.VMEM((ROW_TILE, D_FF), jnp.bfloat16),
            pltpu.VMEM((N_FF_CHUNKS, 2, 8, FF_CHUNK), jnp.float32),
        ],
        compiler_params=pltpu.CompilerParams(
            dimension_semantics=("parallel", "arbitrary"), vmem_limit_bytes=VMEM_LIMIT),
        name=name,
    )(x, o, wo, g.reshape(1, d), wu, cw, cb.reshape(1, 2 * D_FF), wd, gf.reshape(1, d))


def _rope_tables(positions):
    inv_freq = ROPE_THETA ** (-jnp.arange(0, ROT_DIM, 2, dtype=jnp.float32) / ROT_DIM)
    ang = positions.astype(jnp.float32)[..., None] * inv_freq
    cos, sin = jnp.cos(ang), jnp.sin(ang)
    half = ROT_DIM // 2
    rest = HEAD_DIM - ROT_DIM
    shape = ang.shape[:-1]
    ones = jnp.ones(shape + (rest,), jnp.float32)
    z_half = jnp.zeros(shape + (half,), jnp.float32)
    z_rest = jnp.zeros(shape + (rest,), jnp.float32)
    rc = jnp.concatenate([cos, cos, ones], axis=-1)
    ra = jnp.concatenate([-sin, z_half, z_rest], axis=-1)
    rb = jnp.concatenate([z_half, sin, z_rest], axis=-1)
    reps = LANES // HEAD_DIM
    return tuple(jnp.tile(t, (1, 1, reps)) for t in (rc, ra, rb))


def _dup_heads(w):
    d = w.shape[0]
    w = w.reshape(d, N_KV_HEADS, 1, HEAD_DIM)
    return jnp.broadcast_to(w, (d, N_KV_HEADS, LANES // HEAD_DIM, HEAD_DIM)).reshape(d, N_KV_HEADS * LANES)


def kernel(x, positions, attn_norm, w_qkv_a, w_o_a, kv_norm, w_kv_b, w_q_b, sinks_b, w_o_b, ffn_norm,
           w_up, conv_w, conv_b, w_down, final_norm):
    assert x.shape[1] % ROW_TILE == 0 and x.shape[1] % SWA_Q_TILE == 0 and x.shape[2] == D_MODEL
    rope = _rope_tables(positions)
    scale = HEAD_DIM ** -0.5
    qw = N_HEADS * HEAD_DIM
    kvw = N_KV_HEADS * HEAD_DIM
    kv = None
    for l in range(DEPTH):
        if l < N_A:
            qkv, kmean = _project(
                x, attn_norm[l], w_qkv_a[l].astype(jnp.bfloat16), rope, rope_cols=2 * qw, scale_cols=qw,
                scale=scale, kmean_cols=(qw, 2 * qw), name=f"proj_qkv_{l}")
            o = _moba_attention(qkv, kmean)
            wo = w_o_a[l]
        else:
            i = l - N_A
            if l == N_A:
                w_kv = jnp.concatenate([_dup_heads(w_kv_b[:, :kvw]), _dup_heads(w_kv_b[:, kvw:])], axis=1)
                kv = _project(x, kv_norm, w_kv.astype(jnp.bfloat16), rope, rope_cols=N_KV_HEADS * LANES,
                              name="proj_kv")
            q = _project(x, attn_norm[l], w_q_b[i].astype(jnp.bfloat16), rope, rope_cols=qw, scale_cols=qw,
                         scale=scale, name=f"proj_q_{l}")
            o = _swa_attention(q, kv, sinks_b[i])
            wo = w_o_b[i]
        x = _attn_out_ffn(x, o, wo.astype(jnp.bfloat16), ffn_norm[l], w_up[l].astype(jnp.bfloat16), conv_w[l],
                          conv_b[l], w_down[l].astype(jnp.bfloat16), final_norm,
                          final=(l == DEPTH - 1), name=f"attn_out_ffn_{l}")
    return x
```
